```python
import jax
import jax.numpy as jnp
from jax import lax
import numpy as np

D_MODEL = 1024
BATCH = 2
SEQ = 16384
DEPTH = 2
DEC_BATCH = 8
DEC_SEQ = 16
PAST_LEN = 1024

CHUNK = 64
N_A = DEPTH // 2
N_B = DEPTH - N_A
GMLP_CHUNK = 128
GMLP_HALF = 2 * D_MODEL
GMLP_GROUPS = 8
GMLP_GDIM = GMLP_HALF // GMLP_GROUPS
SB_HEADS = 16
SB_HEAD_DIM = D_MODEL // SB_HEADS
SB_QBLOCK = 128
D_FF = ((8 * D_MODEL // 3 + 127) // 128) * 128
CONV_W = 3
N_ADA = 6
EPS = 1e-6

kernel_name = "yoco_gmlp_stickbreaking_stream_step"


def _rmsnorm(x, g):
    xf = x.astype(jnp.float32)
    y = xf * lax.rsqrt(jnp.mean(xf * xf, axis=-1, keepdims=True) + EPS)
    return (y * g.astype(jnp.float32)).astype(x.dtype)


def _modulate(h, shift, scale):
    return h * (1.0 + scale[:, None, :]) + shift[:, None, :]


def _ada(c, w, b, n):
    return jnp.split(jax.nn.silu(c) @ w + b, n, axis=-1)


def _gmlp_mixer(h, w_in, g_v, w_s, b_s, w_out):
    b, t, _ = h.shape
    u, v = jnp.split(jax.nn.gelu(h @ w_in), 2, axis=-1)
    v = _rmsnorm(v, g_v)
    n = min(t, GMLP_CHUNK)
    nc = t // n
    tri = jnp.tril(jnp.ones((n, n), dtype=bool))
    ws = jnp.where(tri, w_s[:, :n, :n], 0.0)
    vb = v.reshape(b, nc, n, GMLP_GROUPS, GMLP_GDIM)
    mixed = jnp.einsum('gts,bnsgc->bntgc', ws, vb) + b_s[:, :n].T[:, :, None]
    y = (u * mixed.reshape(b, t, GMLP_HALF)) @ w_out
    return y, v


def _conv_ffn(h, conv_hist, w_up, conv_w, conv_b, w_down):
    t = h.shape[1]
    a, g = jnp.split(h @ w_up, 2, axis=-1)
    gp = jnp.concatenate([conv_hist.astype(g.dtype), g], axis=1)
    gc = conv_b
    for i in range(CONV_W):
        gc = gc + conv_w[i] * gp[:, i:i + t]
    y = (jax.nn.silu(gc) * a) @ w_down
    return y, gp[:, -(CONV_W - 1):]


def _sb_block(q, k, v, t_pos, s_pos, acc_log):
    z = jnp.einsum('bhtd,bhsd->bhts', q, k) * (SB_HEAD_DIM ** -0.5)
    mask = s_pos[None, :] < t_pos[:, None]
    log_1mb = jnp.where(mask, jax.nn.log_sigmoid(-z), 0.0)
    later = lax.cumsum(log_1mb, axis=3, reverse=True) - log_1mb
    log_a = jax.nn.log_sigmoid(z) + later + acc_log[..., None]
    a = jnp.where(mask, jnp.exp(log_a), 0.0)
    return jnp.einsum('bhts,bhsd->bhtd', a, v), jnp.sum(log_1mb, axis=-1)


def _sb_prompt(q, k, v):
    b, s, hh, d = q.shape
    nb = s // SB_QBLOCK
    def to_blocks(a):
        return a.astype(jnp.float32).reshape(b, nb, SB_QBLOCK, hh, d).transpose(1, 0, 3, 2, 4)
    qb, kb, vb = to_blocks(q), to_blocks(k), to_blocks(v)
    offs = jnp.arange(SB_QBLOCK)

    def one_query_block(args):
        qi, q_blk = args
        t_pos = qi * SB_QBLOCK + offs
        def body(i, carry):
            o, acc = carry
            ki = qi - i
            do, bs = _sb_block(q_blk, kb[ki], vb[ki], t_pos, ki * SB_QBLOCK + offs, acc)
            return o + do, acc + bs
        init = (jnp.zeros_like(q_blk), jnp.zeros(q_blk.shape[:-1], jnp.float32))
        o, _ = lax.fori_loop(0, qi + 1, body, init)
        return o

    o = lax.map(one_query_block, (jnp.arange(nb), qb))
    return o.transpose(1, 0, 3, 2, 4).reshape(b, s, hh * d).astype(q.dtype)


def _sb_sample(q, k_all, v_all, past):
    b, t, hh, d = q.shape
    def tr(a):
        return a.astype(jnp.float32).transpose(0, 2, 1, 3)
    o, _ = _sb_block(tr(q), tr(k_all), tr(v_all), past + jnp.arange(t), jnp.arange(k_all.shape[1]),
                     jnp.zeros((b, hh, t), jnp.float32))
    return o.transpose(0, 2, 1, 3).reshape(b, t, hh * d).astype(q.dtype)


def _forward(x, c, conv_hist, cache_k, cache_v, ada_w, ada_b, norm_mix, norm_ffn,
             gmlp_w_in, gmlp_norm_v, gmlp_w_s, gmlp_b_s, gmlp_w_out,
             kv_ada_w, kv_ada_b, kv_norm, w_kv, sb_w_q, sb_w_o,
             ffn_w_up, ffn_conv_w, ffn_conv_b, ffn_w_down, final_norm):
    b, t, _ = x.shape
    hd = SB_HEADS * SB_HEAD_DIM
    k_sh = v_sh = None
    conv_rows, v_rows = [], []
    for layer in range(DEPTH):
        sh1, sc1, g1, sh2, sc2, g2 = _ada(c, ada_w[layer], ada_b[layer], N_ADA)
        h = _modulate(_rmsnorm(x, norm_mix[layer]), sh1, sc1)
        if layer < N_A:
            y, vr = _gmlp_mixer(h, gmlp_w_in[layer], gmlp_norm_v[layer], gmlp_w_s[layer],
                                gmlp_b_s[layer], gmlp_w_out[layer])
            v_rows.append(vr)
        else:
            j = layer - N_A
            q = (h @ sb_w_q[j]).reshape(b, t, SB_HEADS, SB_HEAD_DIM)
            if cache_k is None:
                o = _sb_prompt(q, k_sh, v_sh)
            else:
                o = _sb_sample(q, jnp.concatenate([cache_k.astype(k_sh.dtype), k_sh], axis=1),
                               jnp.concatenate([cache_v.astype(v_sh.dtype), v_sh], axis=1),
                               cache_k.shape[1])
            y = o @ sb_w_o[j]
        x = x + g1[:, None, :] * y
        h = _modulate(_rmsnorm(x, norm_ffn[layer]), sh2, sc2)
        y, rows = _conv_ffn(h, conv_hist[layer], ffn_w_up[layer], ffn_conv_w[layer],
                            ffn_conv_b[layer], ffn_w_down[layer])
        conv_rows.append(rows)
        x = x + g2[:, None, :] * y
        if layer == N_A - 1:
            shk, sck = _ada(c, kv_ada_w, kv_ada_b, 2)
            kv = _modulate(_rmsnorm(x, kv_norm), shk, sck) @ w_kv
            k_sh = kv[..., :hd].reshape(b, t, SB_HEADS, SB_HEAD_DIM)
            v_sh = kv[..., hd:].reshape(b, t, SB_HEADS, SB_HEAD_DIM)
    return _rmsnorm(x, final_norm), k_sh, v_sh, jnp.stack(conv_rows), jnp.stack(v_rows)


def setup_inputs(seed: int = 0) -> dict:
    key = jax.random.key(seed)
    ks = jax.random.split(key, 32)
    def nrm(k, shape, s=1.0):
        return s * jax.random.normal(k, shape, jnp.float32)
    D, U, F, HD = D_MODEL, GMLP_HALF, D_FF, SB_HEADS * SB_HEAD_DIM
    return {
        "x_prompt": nrm(ks[0], (BATCH, SEQ, D)),
        "x_sample": nrm(ks[1], (DEC_BATCH, DEC_SEQ, D)),
        "c_prompt": nrm(ks[2], (BATCH, D)),
        "c_sample": nrm(ks[3], (DEC_BATCH, D)),
        "cache_k": nrm(ks[4], (DEC_BATCH, PAST_LEN, SB_HEADS, SB_HEAD_DIM)),
        "cache_v": nrm(ks[5], (DEC_BATCH, PAST_LEN, SB_HEADS, SB_HEAD_DIM)),
        "state_conv": nrm(ks[6], (DEPTH, DEC_BATCH, CONV_W - 1, F)),
        "ada_w": nrm(ks[7], (DEPTH, D, N_ADA * D), 0.5 * D ** -0.5),
        "ada_b": nrm(ks[8], (DEPTH, N_ADA * D), 0.01),
        "norm_mix": 1.0 + nrm(ks[9], (DEPTH, D), 0.01),
        "norm_ffn": 1.0 + nrm(ks[10], (DEPTH, D), 0.01),
        "gmlp_w_in": nrm(ks[11], (N_A, D, 2 * U), D ** -0.5),
        "gmlp_norm_v": 1.0 + nrm(ks[12], (N_A, U), 0.01),
        "gmlp_w_s": nrm(ks[13], (N_A, GMLP_GROUPS, GMLP_CHUNK, GMLP_CHUNK), GMLP_CHUNK ** -0.5),
        "gmlp_b_s": 1.0 + nrm(ks[14], (N_A, GMLP_GROUPS, GMLP_CHUNK), 0.01),
        "gmlp_w_out": nrm(ks[15], (N_A, U, D), U ** -0.5),
        "kv_ada_w": nrm(ks[16], (D, 2 * D), 0.5 * D ** -0.5),
        "kv_ada_b": nrm(ks[17], (2 * D,), 0.01),
        "kv_norm": 1.0 + nrm(ks[18], (D,), 0.01),
        "w_kv": nrm(ks[19], (D, 2 * HD), D ** -0.5),
        "sb_w_q": nrm(ks[20], (N_B, D, HD), D ** -0.5),
        "sb_w_o": nrm(ks[21], (N_B, HD, D), HD ** -0.5),
        "ffn_w_up": nrm(ks[22], (DEPTH, D, 2 * F), D ** -0.5),
        "ffn_conv_w": nrm(ks[23], (DEPTH, CONV_W, F), CONV_W ** -0.5),
        "ffn_conv_b": nrm(ks[24], (DEPTH, F), 0.01),
        "ffn_w_down": nrm(ks[25], (DEPTH, F, D), F ** -0.5),
        "final_norm": 1.0 + nrm(ks[26], (D,), 0.01),
    }


def reference(x_prompt, x_sample, c_prompt, c_sample, cache_k, cache_v, state_conv,
              ada_w, ada_b, norm_mix, norm_ffn,
              gmlp_w_in, gmlp_norm_v, gmlp_w_s, gmlp_b_s, gmlp_w_out,
              kv_ada_w, kv_ada_b, kv_norm, w_kv, sb_w_q, sb_w_o,
              ffn_w_up, ffn_conv_w, ffn_conv_b, ffn_w_down, final_norm):
    weights = (ada_w, ada_b, norm_mix, norm_ffn,
               gmlp_w_in, gmlp_norm_v, gmlp_w_s, gmlp_b_s, gmlp_w_out,
               kv_ada_w, kv_ada_b, kv_norm, w_kv, sb_w_q, sb_w_o,
               ffn_w_up, ffn_conv_w, ffn_conv_b, ffn_w_down, final_norm)
    b = x_prompt.shape[0]
    conv_zero = jnp.zeros((DEPTH, b, CONV_W - 1, D_FF), x_prompt.dtype)
    y_prompt, k_prompt, v_prompt, conv_prompt, _ = _forward(
        x_prompt, c_prompt, conv_zero, None, None, *weights)
    y_sample, k_sample, v_sample, conv_sample, gmlp_v_sample = _forward(
        x_sample, c_sample, state_conv, cache_k, cache_v, *weights)
    return (y_prompt, y_sample, k_prompt, v_prompt, k_sample, v_sample,
            conv_prompt, conv_sample, gmlp_v_sample)
```

```python
import functools

import jax
import jax.numpy as jnp
from jax import lax
from jax.experimental import pallas as pl
from jax.experimental.pallas import tpu as pltpu

F32 = jnp.float32
BF16 = jnp.bfloat16

EPS = 1e-6
GMLP_CHUNK = 128
GMLP_GROUPS = 8
SB_HEADS = 16
SB_HEAD_DIM = 64
CONV_W = 3
N_ADA = 6

LANES = 128
HEADS_PER_STEP = LANES // SB_HEAD_DIM
ADA_ROWS = 16
ADA_TILE_N = 1536
ATTN_BLOCK = 128
ATTN_DEAD_LOG = -110.0
VMEM_LIMIT_BYTES = 56 * 1024 * 1024


def _params(semantics):
    return pltpu.CompilerParams(dimension_semantics=semantics, vmem_limit_bytes=VMEM_LIMIT_BYTES)


def _resident(shape):
    zeros = (0,) * len(shape)
    return pl.BlockSpec(shape, lambda *_: zeros, pipeline_mode=pl.Buffered(1))


def _rms(x, g):
    return x * lax.rsqrt(jnp.mean(x * x, axis=-1, keepdims=True) + EPS) * g


def _dot(a, b):
    return jnp.dot(a, b, preferred_element_type=F32)


def _ada_kernel(c_ref, w_ref, b_ref, o_ref):
    c = c_ref[...]
    s = (c * jax.nn.sigmoid(c)).astype(BF16)
    o_ref[0] = _dot(s, w_ref[0].astype(BF16)) + b_ref[0]


def _ada_call(c_rows, w, b):
    n_layers, d, n = w.shape
    tn = ADA_TILE_N if n % ADA_TILE_N == 0 else n
    return pl.pallas_call(
        _ada_kernel,
        grid=(n_layers, n // tn),
        in_specs=[
            pl.BlockSpec((ADA_ROWS, d), lambda l, j: (0, 0)),
            pl.BlockSpec((1, d, tn), lambda l, j: (l, 0, j)),
            pl.BlockSpec((1, 1, tn), lambda l, j: (l, 0, j)),
        ],
        out_specs=pl.BlockSpec((1, ADA_ROWS, tn), lambda l, j: (l, 0, j)),
        out_shape=jax.ShapeDtypeStruct((n_layers, ADA_ROWS, n), F32),
        compiler_params=_params(("parallel", "parallel")),
        name="ada_proj",
    )(c_rows, w, b.reshape(n_layers, 1, n))


def _gmlp_kernel(x_ref, mod_ref, gn_ref, win_ref, gv_ref, ws_ref, bs_ref, wout_ref, *rest,
                 chunk, emit_v):
    if emit_v:
        xo_ref, vo_ref, z_ref = rest
    else:
        xo_ref, z_ref = rest
    x = x_ref[0]
    mod = mod_ref[0]
    shift, scale, gate = mod[0:1], mod[1:2], mod[2:3]
    h = (_rms(x, gn_ref[...]) * (1.0 + scale) + shift).astype(BF16)
    uv = jax.nn.gelu(_dot(h, win_ref[...]))
    half = uv.shape[1] // 2
    u = uv[:, :half]
    v = _rms(uv[:, half:], gv_ref[...])
    if emit_v:
        vo_ref[0] = v
    vb = v.astype(BF16)
    gdim = half // GMLP_GROUPS
    row = lax.broadcasted_iota(jnp.int32, (chunk, chunk), 0)
    col = lax.broadcasted_iota(jnp.int32, (chunk, chunk), 1)
    causal = row >= col
    for g in range(GMLP_GROUPS):
        w = jnp.where(causal, ws_ref[g], 0.0).astype(BF16)
        bias = bs_ref[:, g:g + 1]
        cs = slice(g * gdim, (g + 1) * gdim)
        for c in range(x.shape[0] // chunk):
            rs = slice(c * chunk, (c + 1) * chunk)
            mixed = _dot(w, vb[rs, cs]) + bias
            z_ref[rs, cs] = (u[rs, cs] * mixed).astype(BF16)
    y = _dot(z_ref[...], wout_ref[...])
    xo_ref[0] = x + gate * y


def _gmlp_call(x, mod, gn, w_in, g_v, w_s, b_s_t, w_out, *, tm, chunk, emit_v):
    b, t, d = x.shape
    u2 = w_in.shape[1]
    half = u2 // 2
    x_spec = pl.BlockSpec((1, tm, d), lambda i, j: (i, j, 0))
    out_specs = [x_spec]
    out_shape = [jax.ShapeDtypeStruct((b, t, d), F32)]
    if emit_v:
        out_specs.append(pl.BlockSpec((1, tm, half), lambda i, j: (i, j, 0)))
        out_shape.append(jax.ShapeDtypeStruct((b, t, half), F32))
    return pl.pallas_call(
        functools.partial(_gmlp_kernel, chunk=chunk, emit_v=emit_v),
        grid=(b, t // tm),
        in_specs=[
            x_spec,
            pl.BlockSpec((1, 8, d), lambda i, j: (i, 0, 0)),
            _resident((1, d)),
            _resident((d, u2)),
            _resident((1, half)),
            _resident((GMLP_GROUPS, chunk, chunk)),
            _resident((chunk, GMLP_GROUPS)),
            _resident((half, d)),
        ],
        out_specs=out_specs,
        out_shape=out_shape,
        scratch_shapes=[pltpu.VMEM((tm, half), BF16)],
        compiler_params=_params(("parallel", "parallel")),
        name="gmlp_layer",
    )(x, mod, gn, w_in, g_v, w_s, b_s_t, w_out)


def _conv_ffn(x, shift, scale, gate, gn, w_up, conv_w, conv_b, w_down, hist_ref, row_ref):
    tm = x.shape[0]
    h = (_rms(x, gn) * (1.0 + scale) + shift).astype(BF16)
    ag = _dot(h, w_up)
    f = ag.shape[1] // 2
    a = ag[:, :f]
    g = ag[:, f:]
    prev1 = hist_ref[1:2, :]
    prev2 = hist_ref[0:1, :]
    row = lax.broadcasted_iota(jnp.int32, (tm, 1), 0)
    g1 = jnp.where(row == 0, prev1, pltpu.roll(g, 1, axis=0))
    g2 = jnp.where(row == 0, prev2, jnp.where(row == 1, prev1, pltpu.roll(g, 2, axis=0)))
    gc = conv_b + conv_w[0:1] * g2 + conv_w[1:2] * g1 + conv_w[2:3] * g
    last = g[tm - (CONV_W - 1):, :]
    hist_ref[...] = last
    row_ref[0] = last
    y = _dot((gc * jax.nn.sigmoid(gc) * a).astype(BF16), w_down)
    return x + gate * y


def _ffn0_kernel(x_ref, mod_ref, hist_in_ref, gn_ref, wup_ref, cw_ref, cb_ref, wdown_ref,
                 gkv_ref, wkv_ref, gq_ref, wq_ref,
                 xo_ref, row_ref, k_ref, v_ref, kb_ref, vb_ref, qb_ref, hist_ref):
    @pl.when(pl.program_id(1) == 0)
    def _():
        hist_ref[...] = hist_in_ref[0]

    mod = mod_ref[0, 0]
    mod_next = mod_ref[1, 0]
    x = _conv_ffn(x_ref[0], mod[3:4], mod[4:5], mod[5:6], gn_ref[...], wup_ref[...], cw_ref[...],
                  cb_ref[...], wdown_ref[...], hist_ref, row_ref)
    xo_ref[0] = x
    hk = (_rms(x, gkv_ref[...]) * (1.0 + mod[7:8]) + mod[6:7]).astype(BF16)
    kv = _dot(hk, wkv_ref[...])
    hd = kv.shape[1] // 2
    k_ref[0] = kv[:, :hd]
    v_ref[0] = kv[:, hd:]
    kb_ref[0] = kv[:, :hd].astype(BF16)
    vb_ref[0] = kv[:, hd:].astype(BF16)
    hq = (_rms(x, gq_ref[...]) * (1.0 + mod_next[1:2]) + mod_next[0:1]).astype(BF16)
    qb_ref[0] = (_dot(hq, wq_ref[...]) * (SB_HEAD_DIM ** -0.5)).astype(BF16)


def _ffn1_kernel(x_ref, o_ref, mod_ref, hist_in_ref, wo_ref, gn_ref, wup_ref, cw_ref, cb_ref,
                 wdown_ref, gfin_ref, y_ref, row_ref, hist_ref):
    @pl.when(pl.program_id(1) == 0)
    def _():
        hist_ref[...] = hist_in_ref[0]

    mod = mod_ref[0]
    x = x_ref[0] + mod[2:3] * _dot(o_ref[0], wo_ref[...])
    x = _conv_ffn(x, mod[3:4], mod[4:5], mod[5:6], gn_ref[...], wup_ref[...], cw_ref[...],
                  cb_ref[...], wdown_ref[...], hist_ref, row_ref)
    y_ref[0] = _rms(x, gfin_ref[...])


def _ffn0_call(x, mod01, hist, gn, w_up, conv_w, conv_b, w_down, g_kv, w_kv, g_q, w_q, *, tm):
    b, t, d = x.shape
    f = w_down.shape[0]
    hd = w_q.shape[1]
    row_block = lambda last, dt: (pl.BlockSpec((1, tm, last), lambda i, j: (i, j, 0)),
                                  jax.ShapeDtypeStruct((b, t, last), dt))
    outs = [row_block(d, F32),
            (pl.BlockSpec((1, CONV_W - 1, f), lambda i, j: (i, 0, 0)),
             jax.ShapeDtypeStruct((b, CONV_W - 1, f), F32)),
            row_block(hd, F32), row_block(hd, F32),
            row_block(hd, BF16), row_block(hd, BF16), row_block(hd, BF16)]
    return pl.pallas_call(
        _ffn0_kernel,
        grid=(b, t // tm),
        in_specs=[
            pl.BlockSpec((1, tm, d), lambda i, j: (i, j, 0)),
            pl.BlockSpec((2, 1, 8, d), lambda i, j: (0, i, 0, 0)),
            pl.BlockSpec((1, CONV_W - 1, f), lambda i, j: (i, 0, 0)),
            _resident((1, d)), _resident((d, 2 * f)), _resident((CONV_W, f)), _resident((1, f)),
            _resident((f, d)), _resident((1, d)), _resident((d, 2 * hd)), _resident((1, d)),
            _resident((d, hd)),
        ],
        out_specs=[o[0] for o in outs],
        out_shape=[o[1] for o in outs],
        scratch_shapes=[pltpu.VMEM((CONV_W - 1, f), F32)],
        compiler_params=_params(("parallel", "arbitrary")),
        name="ffn0_kv_q",
    )(x, mod01, hist, gn, w_up, conv_w, conv_b, w_down, g_kv, w_kv, g_q, w_q)


def _ffn1_call(x, o, mod, hist, w_o, gn, w_up, conv_w, conv_b, w_down, g_fin, *, tm):
    b, t, d = x.shape
    f = w_down.shape[0]
    hd = w_o.shape[0]
    return pl.pallas_call(
        _ffn1_kernel,
        grid=(b, t // tm),
        in_specs=[
            pl.BlockSpec((1, tm, d), lambda i, j: (i, j, 0)),
            pl.BlockSpec((1, tm, hd), lambda i, j: (i, j, 0)),
            pl.BlockSpec((1, 8, d), lambda i, j: (i, 0, 0)),
            pl.BlockSpec((1, CONV_W - 1, f), lambda i, j: (i, 0, 0)),
            _resident((hd, d)), _resident((1, d)), _resident((d, 2 * f)), _resident((CONV_W, f)),
            _resident((1, f)), _resident((f, d)), _resident((1, d)),
        ],
        out_specs=[pl.BlockSpec((1, tm, d), lambda i, j: (i, j, 0)),
                   pl.BlockSpec((1, CONV_W - 1, f), lambda i, j: (i, 0, 0))],
        out_shape=[jax.ShapeDtypeStruct((b, t, d), F32),
                   jax.ShapeDtypeStruct((b, CONV_W - 1, f), F32)],
        scratch_shapes=[pltpu.VMEM((CONV_W - 1, f), F32)],
        compiler_params=_params(("parallel", "arbitrary")),
        name="ffn1_final",
    )(x, o, mod, hist, w_o, gn, w_up, conv_w, conv_b, w_down, g_fin)


def _attn_kernel(q_ref, k_ref, v_ref, o_ref, *, tq, q_pos0):
    tk = ATTN_BLOCK
    q_start = q_pos0 + pl.program_id(2) * tq
    first_kb = q_start // tk
    q = q_ref[0]
    lane = lax.broadcasted_iota(jnp.int32, (1, LANES), 1)
    head0 = lane < SB_HEAD_DIM
    q_heads = (jnp.where(head0, q, jnp.zeros_like(q)), jnp.where(head0, jnp.zeros_like(q), q))
    t_pos = q_start + lax.broadcasted_iota(jnp.int32, (tq, 1), 0)
    s_off = lax.broadcasted_iota(jnp.int32, (1, tk), 1)
    suffix = (lax.broadcasted_iota(jnp.int32, (tk, tk), 0)
              >= lax.broadcasted_iota(jnp.int32, (tk, tk), 1)).astype(F32).astype(BF16)

    def block(kb, o, accs, masked):
        start = pl.multiple_of(kb * tk, tk)
        k_blk = k_ref[0, pl.ds(start, tk), :]
        v_blk = v_ref[0, pl.ds(start, tk), :]
        if masked:
            mask = (start + s_off) < t_pos
        pvs, new_accs = [], []
        for qh, acc in zip(q_heads, accs):
            z = lax.dot_general(qh, k_blk, (((1,), (1,)), ((), ())), preferred_element_type=F32)
            log_beta = jnp.minimum(z, 0.0) - jnp.log1p(jnp.exp(-jnp.abs(z)))
            l = log_beta - z
            if masked:
                l = jnp.where(mask, l, 0.0)
            l_hi = l.astype(BF16)
            l_lo = (l - l_hi.astype(F32)).astype(BF16)
            incl = _dot(l_hi, suffix) + _dot(l_lo, suffix)
            a = jnp.exp(log_beta + (incl - l) + acc)
            if masked:
                a = jnp.where(mask, a, 0.0)
            pvs.append(_dot(a.astype(BF16), v_blk))
            new_accs.append(acc + incl[:, 0:1])
        o = o + jnp.where(head0, pvs[0], pvs[1])
        return o, tuple(new_accs)

    def alive(accs):
        return (jnp.max(jnp.maximum(accs[0], accs[1])) > ATTN_DEAD_LOG).astype(jnp.int32)

    zero_acc = jnp.zeros((tq, 1), F32)
    o, accs = block(first_kb, jnp.zeros((tq, LANES), F32), (zero_acc, zero_acc), True)

    def cond(carry):
        kb, live = carry[0], carry[1]
        return jnp.logical_and(kb >= 0, live > 0)

    def body(carry):
        kb, _, o, acc0, acc1 = carry
        o, accs = block(kb, o, (acc0, acc1), False)
        return kb - 1, alive(accs), o, accs[0], accs[1]

    carry = lax.while_loop(cond, body, (first_kb - 1, alive(accs), o, accs[0], accs[1]))
    o_ref[0] = carry[2].astype(o_ref.dtype)


def _attn_call(q, k, v, *, tq, q_pos0):
    b, t, hd = q.shape
    s = k.shape[1]
    kv_spec = pl.BlockSpec((1, s, LANES), lambda i, h, j: (i, 0, h))
    q_spec = pl.BlockSpec((1, tq, LANES), lambda i, h, j: (i, j, h))
    return pl.pallas_call(
        functools.partial(_attn_kernel, tq=tq, q_pos0=q_pos0),
        grid=(b, hd // LANES, t // tq),
        in_specs=[q_spec, kv_spec, kv_spec],
        out_specs=q_spec,
        out_shape=jax.ShapeDtypeStruct((b, t, hd), BF16),
        compiler_params=_params(("parallel", "parallel", "parallel")),
        name="sb_attention",
    )(q, k, v)


def _forward(x, mods, conv_hist, cache_k, cache_v, wts, *, tm, chunk, emit_v):
    b, t, d = x.shape
    x1 = _gmlp_call(x, mods[0], wts["norm_mix0"], wts["gmlp_w_in"], wts["gmlp_norm_v"],
                    wts["gmlp_w_s"][:, :chunk, :chunk], wts["gmlp_b_s"][:, :chunk].T,
                    wts["gmlp_w_out"], tm=tm, chunk=chunk, emit_v=emit_v)
    v_rows = None
    if emit_v:
        x1, v_rows = x1
    else:
        (x1,) = x1
    x2, conv0, k, v, kb, vb, qb = _ffn0_call(
        x1, mods, conv_hist[0], wts["norm_ffn0"], wts["ffn_w_up0"], wts["ffn_conv_w"][0],
        wts["ffn_conv_b0"], wts["ffn_w_down0"], wts["kv_norm"], wts["w_kv"], wts["norm_mix1"],
        wts["sb_w_q"], tm=tm)
    if cache_k is None:
        o = _attn_call(qb, kb, vb, tq=ATTN_BLOCK, q_pos0=0)
    else:
        past = cache_k.shape[1]
        pad = -(past + t) % ATTN_BLOCK
        def all_keys(cache, new):
            rows = jnp.concatenate([cache.reshape(b, past, -1).astype(BF16), new], axis=1)
            return jnp.pad(rows, ((0, 0), (0, pad), (0, 0)))
        o = _attn_call(qb, all_keys(cache_k, kb), all_keys(cache_v, vb), tq=t, q_pos0=past)
    y, conv1 = _ffn1_call(x2, o, mods[1], conv_hist[1], wts["sb_w_o"], wts["norm_ffn1"],
                          wts["ffn_w_up1"], wts["ffn_conv_w"][1], wts["ffn_conv_b1"],
                          wts["ffn_w_down1"], wts["final_norm"], tm=tm)
    heads = (b, t, SB_HEADS, SB_HEAD_DIM)
    return y, k.reshape(heads), v.reshape(heads), jnp.stack([conv0, conv1]), v_rows


def kernel(x_prompt, x_sample, c_prompt, c_sample, cache_k, cache_v, state_conv, ada_w, ada_b, norm_mix, norm_ffn, gmlp_w_in, gmlp_norm_v, gmlp_w_s, gmlp_b_s, gmlp_w_out, kv_ada_w, kv_ada_b, kv_norm, w_kv, sb_w_q, sb_w_o, ffn_w_up, ffn_conv_w, ffn_conv_b, ffn_w_down, final_norm):
    depth, d = norm_mix.shape
    assert depth == 2 and gmlp_w_in.shape[0] == 1 and sb_w_q.shape[0] == 1
    bp, tp, _ = x_prompt.shape
    bs, ts, _ = x_sample.shape
    assert bp + bs <= ADA_ROWS

    c_rows = jnp.zeros((ADA_ROWS, d), F32).at[:bp].set(c_prompt).at[bp:bp + bs].set(c_sample)
    ada = _ada_call(c_rows, ada_w, ada_b)
    kv_ada = _ada_call(c_rows, kv_ada_w[None], kv_ada_b[None])
    ada = ada.reshape(depth, ADA_ROWS, N_ADA, d)
    kv_rows = jnp.broadcast_to(kv_ada.reshape(1, ADA_ROWS, 2, d), (depth, ADA_ROWS, 2, d))
    mods = jnp.concatenate([ada, kv_rows], axis=2)

    row = lambda a: a.reshape(1, -1)
    wts = {
        "norm_mix0": row(norm_mix[0]), "norm_mix1": row(norm_mix[1]),
        "norm_ffn0": row(norm_ffn[0]), "norm_ffn1": row(norm_ffn[1]),
        "gmlp_w_in": gmlp_w_in[0].astype(BF16), "gmlp_norm_v": row(gmlp_norm_v[0]),
        "gmlp_w_s": gmlp_w_s[0], "gmlp_b_s": gmlp_b_s[0], "gmlp_w_out": gmlp_w_out[0].astype(BF16),
        "kv_norm": row(kv_norm), "w_kv": w_kv.astype(BF16),
        "sb_w_q": sb_w_q[0].astype(BF16), "sb_w_o": sb_w_o[0].astype(BF16),
        "ffn_w_up0": ffn_w_up[0].astype(BF16), "ffn_w_up1": ffn_w_up[1].astype(BF16),
        "ffn_conv_w": ffn_conv_w, "ffn_conv_b0": row(ffn_conv_b[0]), "ffn_conv_b1": row(ffn_conv_b[1]),
        "ffn_w_down0": ffn_w_down[0].astype(BF16), "ffn_w_down1": ffn_w_down[1].astype(BF16),
        "final_norm": row(final_norm),
    }

    conv_zero = jnp.zeros((depth, bp, CONV_W - 1, state_conv.shape[-1]), F32)
    tm_prompt = min(tp, 256)
    y_p, k_p, v_p, conv_p, _ = _forward(
        x_prompt, mods[:, :bp], conv_zero, None, None, wts,
        tm=tm_prompt, chunk=min(tp, GMLP_CHUNK), emit_v=False)
    y_s, k_s, v_s, conv_s, gv_s = _forward(
        x_sample, mods[:, bp:bp + bs], state_conv, cache_k, cache_v, wts,
        tm=ts, chunk=min(ts, GMLP_CHUNK), emit_v=True)
    return (y_p, y_s, k_p, v_p, k_s, v_s, conv_p, conv_s, gv_s[None])
```

```python
import functools

import jax
import jax.numpy as jnp
from jax import lax
from jax.experimental import pallas as pl
from jax.experimental.pallas import tpu as pltpu

F32 = jnp.float32
BF16 = jnp.bfloat16

EPS = 1e-6
GMLP_CHUNK = 128
GMLP_GROUPS = 8
SB_HEADS = 16
SB_HEAD_DIM = 64
CONV_W = 3
N_ADA = 6

LANES = 128
HEADS_PER_STEP = LANES // SB_HEAD_DIM
ADA_ROWS = 16
ADA_TILE_N = 1536
ATTN_BLOCK = 128
ATTN_WINDOW = 3 * ATTN_BLOCK
ATTN_DEAD_NEG_LOG = 110.0
VMEM_LIMIT_BYTES = 56 * 1024 * 1024


def _params(semantics):
    return pltpu.CompilerParams(dimension_semantics=semantics, vmem_limit_bytes=VMEM_LIMIT_BYTES)


def _resident(shape):
    zeros = (0,) * len(shape)
    return pl.BlockSpec(shape, lambda *_: zeros, pipeline_mode=pl.Buffered(1))


def _rms(x, g):
    return x * lax.rsqrt(jnp.mean(x * x, axis=-1, keepdims=True) + EPS) * g


def _dot(a, b):
    return jnp.dot(a, b, preferred_element_type=F32)


def _ada_kernel(c_ref, w_ref, b_ref, o_ref):
    c = c_ref[...]
    s = (c * jax.nn.sigmoid(c)).astype(BF16)
    o_ref[0] = _dot(s, w_ref[0].astype(BF16)) + b_ref[0]


def _ada_call(c_rows, w, b):
    n_layers, d, n = w.shape
    tn = ADA_TILE_N if n % ADA_TILE_N == 0 else n
    return pl.pallas_call(
        _ada_kernel,
        grid=(n_layers, n // tn),
        in_specs=[
            pl.BlockSpec((ADA_ROWS, d), lambda l, j: (0, 0)),
            pl.BlockSpec((1, d, tn), lambda l, j: (l, 0, j)),
            pl.BlockSpec((1, 1, tn), lambda l, j: (l, 0, j)),
        ],
        out_specs=pl.BlockSpec((1, ADA_ROWS, tn), lambda l, j: (l, 0, j)),
        out_shape=jax.ShapeDtypeStruct((n_layers, ADA_ROWS, n), F32),
        compiler_params=_params(("parallel", "parallel")),
        name="ada_proj",
    )(c_rows, w, b.reshape(n_layers, 1, n))


def _gmlp_kernel(x_ref, mod_ref, gn_ref, win_ref, gv_ref, ws_ref, bs_ref, wout_ref, *rest,
                 chunk, emit_v):
    if emit_v:
        xo_ref, vo_ref, z_ref = rest
    else:
        xo_ref, z_ref = rest
    x = x_ref[0]
    mod = mod_ref[0]
    shift, scale, gate = mod[0:1], mod[1:2], mod[2:3]
    h = (_rms(x, gn_ref[...]) * (1.0 + scale) + shift).astype(BF16)
    uv = jax.nn.gelu(_dot(h, win_ref[...]))
    half = uv.shape[1] // 2
    u = uv[:, :half]
    v = _rms(uv[:, half:], gv_ref[...])
    if emit_v:
        vo_ref[0] = v
    vb = v.astype(BF16)
    gdim = half // GMLP_GROUPS
    row = lax.broadcasted_iota(jnp.int32, (chunk, chunk), 0)
    col = lax.broadcasted_iota(jnp.int32, (chunk, chunk), 1)
    causal = row >= col
    for g in range(GMLP_GROUPS):
        w = jnp.where(causal, ws_ref[g], 0.0).astype(BF16)
        bias = bs_ref[:, g:g + 1]
        cs = slice(g * gdim, (g + 1) * gdim)
        for c in range(x.shape[0] // chunk):
            rs = slice(c * chunk, (c + 1) * chunk)
            mixed = _dot(w, vb[rs, cs]) + bias
            z_ref[rs, cs] = (u[rs, cs] * mixed).astype(BF16)
    y = _dot(z_ref[...], wout_ref[...])
    xo_ref[0] = x + gate * y


def _gmlp_call(x, mod, gn, w_in, g_v, w_s, b_s_t, w_out, *, tm, chunk, emit_v):
    b, t, d = x.shape
    u2 = w_in.shape[1]
    half = u2 // 2
    x_spec = pl.BlockSpec((1, tm, d), lambda i, j: (i, j, 0))
    out_specs = [x_spec]
    out_shape = [jax.ShapeDtypeStruct((b, t, d), F32)]
    if emit_v:
        out_specs.append(pl.BlockSpec((1, tm, half), lambda i, j: (i, j, 0)))
        out_shape.append(jax.ShapeDtypeStruct((b, t, half), F32))
    return pl.pallas_call(
        functools.partial(_gmlp_kernel, chunk=chunk, emit_v=emit_v),
        grid=(b, t // tm),
        in_specs=[
            x_spec,
            pl.BlockSpec((1, 8, d), lambda i, j: (i, 0, 0)),
            _resident((1, d)),
            _resident((d, u2)),
            _resident((1, half)),
            _resident((GMLP_GROUPS, chunk, chunk)),
            _resident((chunk, GMLP_GROUPS)),
            _resident((half, d)),
        ],
        out_specs=out_specs,
        out_shape=out_shape,
        scratch_shapes=[pltpu.VMEM((tm, half), BF16)],
        compiler_params=_params(("parallel", "parallel")),
        name="gmlp_layer",
    )(x, mod, gn, w_in, g_v, w_s, b_s_t, w_out)


def _conv_ffn(x, shift, scale, gate, gn, w_up, conv_w, conv_b, w_down, hist_ref, row_ref):
    tm = x.shape[0]
    h = (_rms(x, gn) * (1.0 + scale) + shift).astype(BF16)
    ag = _dot(h, w_up)
    f = ag.shape[1] // 2
    a = ag[:, :f]
    g = ag[:, f:]
    prev1 = hist_ref[1:2, :]
    prev2 = hist_ref[0:1, :]
    row = lax.broadcasted_iota(jnp.int32, (tm, 1), 0)
    g1 = jnp.where(row == 0, prev1, pltpu.roll(g, 1, axis=0))
    g2 = jnp.where(row == 0, prev2, jnp.where(row == 1, prev1, pltpu.roll(g, 2, axis=0)))
    gc = conv_b + conv_w[0:1] * g2 + conv_w[1:2] * g1 + conv_w[2:3] * g
    last = g[tm - (CONV_W - 1):, :]
    hist_ref[...] = last
    row_ref[0] = last
    y = _dot((gc * jax.nn.sigmoid(gc) * a).astype(BF16), w_down)
    return x + gate * y


def _ffn0_kernel(x_ref, mod_ref, hist_in_ref, gn_ref, wup_ref, cw_ref, cb_ref, wdown_ref,
                 gkv_ref, wkv_ref, gq_ref, wq_ref,
                 xo_ref, row_ref, k_ref, v_ref, kb_ref, vb_ref, qb_ref, hist_ref):
    @pl.when(pl.program_id(1) == 0)
    def _():
        hist_ref[...] = hist_in_ref[0]

    mod = mod_ref[0, 0]
    mod_next = mod_ref[1, 0]
    x = _conv_ffn(x_ref[0], mod[3:4], mod[4:5], mod[5:6], gn_ref[...], wup_ref[...], cw_ref[...],
                  cb_ref[...], wdown_ref[...], hist_ref, row_ref)
    xo_ref[0] = x
    hk = (_rms(x, gkv_ref[...]) * (1.0 + mod[7:8]) + mod[6:7]).astype(BF16)
    kv = _dot(hk, wkv_ref[...])
    hd = kv.shape[1] // 2
    k_ref[0] = kv[:, :hd]
    v_ref[0] = kv[:, hd:]
    kb_ref[0] = kv[:, :hd].astype(BF16)
    vb_ref[0] = kv[:, hd:].astype(BF16)
    hq = (_rms(x, gq_ref[...]) * (1.0 + mod_next[1:2]) + mod_next[0:1]).astype(BF16)
    qb_ref[0] = (_dot(hq, wq_ref[...]) * (SB_HEAD_DIM ** -0.5)).astype(BF16)


def _ffn1_kernel(x_ref, o_ref, mod_ref, hist_in_ref, wo_ref, gn_ref, wup_ref, cw_ref, cb_ref,
                 wdown_ref, gfin_ref, y_ref, row_ref, hist_ref):
    @pl.when(pl.program_id(1) == 0)
    def _():
        hist_ref[...] = hist_in_ref[0]

    mod = mod_ref[0]
    x = x_ref[0] + mod[2:3] * _dot(o_ref[0], wo_ref[...])
    x = _conv_ffn(x, mod[3:4], mod[4:5], mod[5:6], gn_ref[...], wup_ref[...], cw_ref[...],
                  cb_ref[...], wdown_ref[...], hist_ref, row_ref)
    y_ref[0] = _rms(x, gfin_ref[...])


def _ffn0_call(x, mod01, hist, gn, w_up, conv_w, conv_b, w_down, g_kv, w_kv, g_q, w_q, *, tm):
    b, t, d = x.shape
    f = w_down.shape[0]
    hd = w_q.shape[1]
    row_block = lambda last, dt: (pl.BlockSpec((1, tm, last), lambda i, j: (i, j, 0)),
                                  jax.ShapeDtypeStruct((b, t, last), dt))
    outs = [row_block(d, F32),
            (pl.BlockSpec((1, CONV_W - 1, f), lambda i, j: (i, 0, 0)),
             jax.ShapeDtypeStruct((b, CONV_W - 1, f), F32)),
            row_block(hd, F32), row_block(hd, F32),
            row_block(hd, BF16), row_block(hd, BF16), row_block(hd, BF16)]
    return pl.pallas_call(
        _ffn0_kernel,
        grid=(b, t // tm),
        in_specs=[
            pl.BlockSpec((1, tm, d), lambda i, j: (i, j, 0)),
            pl.BlockSpec((2, 1, 8, d), lambda i, j: (0, i, 0, 0)),
            pl.BlockSpec((1, CONV_W - 1, f), lambda i, j: (i, 0, 0)),
            _resident((1, d)), _resident((d, 2 * f)), _resident((CONV_W, f)), _resident((1, f)),
            _resident((f, d)), _resident((1, d)), _resident((d, 2 * hd)), _resident((1, d)),
            _resident((d, hd)),
        ],
        out_specs=[o[0] for o in outs],
        out_shape=[o[1] for o in outs],
        scratch_shapes=[pltpu.VMEM((CONV_W - 1, f), F32)],
        compiler_params=_params(("parallel", "arbitrary")),
        name="ffn0_kv_q",
    )(x, mod01, hist, gn, w_up, conv_w, conv_b, w_down, g_kv, w_kv, g_q, w_q)


def _ffn1_call(x, o, mod, hist, w_o, gn, w_up, conv_w, conv_b, w_down, g_fin, *, tm):
    b, t, d = x.shape
    f = w_down.shape[0]
    hd = w_o.shape[0]
    return pl.pallas_call(
        _ffn1_kernel,
        grid=(b, t // tm),
        in_specs=[
            pl.BlockSpec((1, tm, d), lambda i, j: (i, j, 0)),
            pl.BlockSpec((1, tm, hd), lambda i, j: (i, j, 0)),
            pl.BlockSpec((1, 8, d), lambda i, j: (i, 0, 0)),
            pl.BlockSpec((1, CONV_W - 1, f), lambda i, j: (i, 0, 0)),
            _resident((hd, d)), _resident((1, d)), _resident((d, 2 * f)), _resident((CONV_W, f)),
            _resident((1, f)), _resident((f, d)), _resident((1, d)),
        ],
        out_specs=[pl.BlockSpec((1, tm, d), lambda i, j: (i, j, 0)),
                   pl.BlockSpec((1, CONV_W - 1, f), lambda i, j: (i, 0, 0))],
        out_shape=[jax.ShapeDtypeStruct((b, t, d), F32),
                   jax.ShapeDtypeStruct((b, CONV_W - 1, f), F32)],
        scratch_shapes=[pltpu.VMEM((CONV_W - 1, f), F32)],
        compiler_params=_params(("parallel", "arbitrary")),
        name="ffn1_final",
    )(x, o, mod, hist, w_o, gn, w_up, conv_w, conv_b, w_down, g_fin)


def _attn_kernel(q_ref, k_ref, v_ref, sfx_ref, o_ref, *, tq, q_pos0):
    tk, win = ATTN_BLOCK, ATTN_WINDOW
    q_start = q_pos0 + pl.program_id(2) * tq
    win_end = (q_start // tk + 1) * tk
    win_start = jnp.maximum(win_end - win, 0)
    q = q_ref[0]
    lane = lax.broadcasted_iota(jnp.int32, (1, LANES), 1)
    head0 = lane < SB_HEAD_DIM
    zero = jnp.zeros_like(q)
    q2 = jnp.concatenate([jnp.where(head0, q, zero), jnp.where(head0, zero, q)], axis=0)
    t_iota = lax.broadcasted_iota(jnp.int32, (tq, 1), 0)
    t_pos = q_start + jnp.concatenate([t_iota, t_iota], axis=0)

    def block(start, width, o, nacc, masked):
        k_blk = k_ref[0, pl.ds(start, width), :]
        v_blk = v_ref[0, pl.ds(start, width), :]
        z = lax.dot_general(q2, k_blk, (((1,), (1,)), ((), ())), preferred_element_type=F32)
        nl = jnp.maximum(z, 0.0) + jnp.log(1.0 + jnp.exp(-jnp.abs(z)))
        if masked:
            mask = (start + lax.broadcasted_iota(jnp.int32, (1, width), 1)) < t_pos
            nl = jnp.where(mask, nl, 0.0)
        hi = nl.astype(BF16)
        lo = (nl - hi.astype(F32)).astype(BF16)
        sums = _dot(jnp.concatenate([hi, lo], axis=0), sfx_ref[0:width, 0:width])
        incl = sums[:2 * tq] + sums[2 * tq:]
        a = jnp.exp(z - incl - nacc)
        if masked:
            a = jnp.where(mask, a, 0.0)
        pv = _dot(a.astype(BF16), v_blk)
        o = o + jnp.where(head0, pv[:tq], pv[tq:])
        return o, nacc + incl[:, 0:1]

    def alive(nacc):
        return (jnp.min(nacc) < ATTN_DEAD_NEG_LOG).astype(jnp.int32)

    o, nacc = block(pl.multiple_of(win_start, tk), win, jnp.zeros((tq, LANES), F32),
                    jnp.zeros((2 * tq, 1), F32), True)

    def cond(carry):
        return jnp.logical_and(carry[0] >= 0, carry[1] > 0)

    def body(carry):
        kb, _, o, nacc = carry
        o, nacc = block(pl.multiple_of(kb * tk, tk), tk, o, nacc, False)
        return kb - 1, alive(nacc), o, nacc

    carry = lax.while_loop(cond, body, (win_start // tk - 1, alive(nacc), o, nacc))
    o_ref[0] = carry[2].astype(o_ref.dtype)


def _attn_call(q, k, v, *, tq, q_pos0):
    b, t, hd = q.shape
    s = k.shape[1]
    assert ATTN_BLOCK % tq == 0 and q_pos0 % tq == 0 and s % ATTN_BLOCK == 0 and s >= ATTN_WINDOW
    idx = jnp.arange(ATTN_WINDOW)
    suffix = (idx[:, None] >= idx[None, :]).astype(BF16)
    kv_spec = pl.BlockSpec((1, s, LANES), lambda i, h, j: (i, 0, h))
    q_spec = pl.BlockSpec((1, tq, LANES), lambda i, h, j: (i, j, h))
    return pl.pallas_call(
        functools.partial(_attn_kernel, tq=tq, q_pos0=q_pos0),
        grid=(b, hd // LANES, t // tq),
        in_specs=[q_spec, kv_spec, kv_spec, _resident((ATTN_WINDOW, ATTN_WINDOW))],
        out_specs=q_spec,
        out_shape=jax.ShapeDtypeStruct((b, t, hd), BF16),
        compiler_params=_params(("parallel", "parallel", "parallel")),
        name="sb_attention",
    )(q, k, v, suffix)


def _forward(x, mods, conv_hist, cache_k, cache_v, wts, *, tm, chunk, emit_v):
    b, t, d = x.shape
    x1 = _gmlp_call(x, mods[0], wts["norm_mix0"], wts["gmlp_w_in"], wts["gmlp_norm_v"],
                    wts["gmlp_w_s"][:, :chunk, :chunk], wts["gmlp_b_s"][:, :chunk].T,
                    wts["gmlp_w_out"], tm=tm, chunk=chunk, emit_v=emit_v)
    v_rows = None
    if emit_v:
        x1, v_rows = x1
    else:
        (x1,) = x1
    x2, conv0, k, v, kb, vb, qb = _ffn0_call(
        x1, mods, conv_hist[0], wts["norm_ffn0"], wts["ffn_w_up0"], wts["ffn_conv_w"][0],
        wts["ffn_conv_b0"], wts["ffn_w_down0"], wts["kv_norm"], wts["w_kv"], wts["norm_mix1"],
        wts["sb_w_q"], tm=tm)
    if cache_k is None:
        o = _attn_call(qb, kb, vb, tq=ATTN_BLOCK, q_pos0=0)
    else:
        past = cache_k.shape[1]
        pad = -(past + t) % ATTN_BLOCK
        def all_keys(cache, new):
            rows = jnp.concatenate([cache.reshape(b, past, -1).astype(BF16), new], axis=1)
            return jnp.pad(rows, ((0, 0), (0, pad), (0, 0)))
        o = _attn_call(qb, all_keys(cache_k, kb), all_keys(cache_v, vb), tq=t, q_pos0=past)
    y, conv1 = _ffn1_call(x2, o, mods[1], conv_hist[1], wts["sb_w_o"], wts["norm_ffn1"],
                          wts["ffn_w_up1"], wts["ffn_conv_w"][1], wts["ffn_conv_b1"],
                          wts["ffn_w_down1"], wts["final_norm"], tm=tm)
    heads = (b, t, SB_HEADS, SB_HEAD_DIM)
    return y, k.reshape(heads), v.reshape(heads), jnp.stack([conv0, conv1]), v_rows


def kernel(x_prompt, x_sample, c_prompt, c_sample, cache_k, cache_v, state_conv, ada_w, ada_b, norm_mix, norm_ffn, gmlp_w_in, gmlp_norm_v, gmlp_w_s, gmlp_b_s, gmlp_w_out, kv_ada_w, kv_ada_b, kv_norm, w_kv, sb_w_q, sb_w_o, ffn_w_up, ffn_conv_w, ffn_conv_b, ffn_w_down, final_norm):
    depth, d = norm_mix.shape
    assert depth == 2 and gmlp_w_in.shape[0] == 1 and sb_w_q.shape[0] == 1
    bp, tp, _ = x_prompt.shape
    bs, ts, _ = x_sample.shape
    assert bp + bs <= ADA_ROWS

    c_rows = jnp.zeros((ADA_ROWS, d), F32).at[:bp].set(c_prompt).at[bp:bp + bs].set(c_sample)
    ada = _ada_call(c_rows, ada_w, ada_b)
    kv_ada = _ada_call(c_rows, kv_ada_w[None], kv_ada_b[None])
    ada = ada.reshape(depth, ADA_ROWS, N_ADA, d)
    kv_rows = jnp.broadcast_to(kv_ada.reshape(1, ADA_ROWS, 2, d), (depth, ADA_ROWS, 2, d))
    mods = jnp.concatenate([ada, kv_rows], axis=2)

    row = lambda a: a.reshape(1, -1)
    wts = {
        "norm_mix0": row(norm_mix[0]), "norm_mix1": row(norm_mix[1]),
        "norm_ffn0": row(norm_ffn[0]), "norm_ffn1": row(norm_ffn[1]),
        "gmlp_w_in": gmlp_w_in[0].astype(BF16), "gmlp_norm_v": row(gmlp_norm_v[0]),
        "gmlp_w_s": gmlp_w_s[0], "gmlp_b_s": gmlp_b_s[0], "gmlp_w_out": gmlp_w_out[0].astype(BF16),
        "kv_norm": row(kv_norm), "w_kv": w_kv.astype(BF16),
        "sb_w_q": sb_w_q[0].astype(BF16), "sb_w_o": sb_w_o[0].astype(BF16),
        "ffn_w_up0": ffn_w_up[0].astype(BF16), "ffn_w_up1": ffn_w_up[1].astype(BF16),
        "ffn_conv_w": ffn_conv_w, "ffn_conv_b0": row(ffn_conv_b[0]), "ffn_conv_b1": row(ffn_conv_b[1]),
        "ffn_w_down0": ffn_w_down[0].astype(BF16), "ffn_w_down1": ffn_w_down[1].astype(BF16),
        "final_norm": row(final_norm),
    }

    conv_zero = jnp.zeros((depth, bp, CONV_W - 1, state_conv.shape[-1]), F32)
    tm_prompt = min(tp, 256)
    y_p, k_p, v_p, conv_p, _ = _forward(
        x_prompt, mods[:, :bp], conv_zero, None, None, wts,
        tm=tm_prompt, chunk=min(tp, GMLP_CHUNK), emit_v=False)
    y_s, k_s, v_s, conv_s, gv_s = _forward(
        x_sample, mods[:, bp:bp + bs], state_conv, cache_k, cache_v, wts,
        tm=ts, chunk=min(ts, GMLP_CHUNK), emit_v=True)
    return (y_p, y_s, k_p, v_p, k_s, v_s, conv_p, conv_s, gv_s[None])
```

```python
import functools

import jax
import jax.numpy as jnp
from jax import lax
from jax.experimental import pallas as pl
from jax.experimental.pallas import tpu as pltpu

F32 = jnp.float32
BF16 = jnp.bfloat16

EPS = 1e-6
GMLP_CHUNK = 128
GMLP_GROUPS = 8
SB_HEADS = 16
SB_HEAD_DIM = 64
CONV_W = 3
N_ADA = 6

LANES = 128
HEADS_PER_STEP = LANES // SB_HEAD_DIM
ADA_ROWS = 16
ADA_TILE_N = 1536
ATTN_BLOCK = 128
ATTN_WINDOW = 3 * ATTN_BLOCK
ATTN_SUFFIX = 2 * ATTN_BLOCK
ATTN_LANES = 2 * LANES
ATTN_DEAD_NEG_LOG = 110.0
VMEM_LIMIT_BYTES = 56 * 1024 * 1024


def _params(semantics):
    return pltpu.CompilerParams(dimension_semantics=semantics, vmem_limit_bytes=VMEM_LIMIT_BYTES)


def _resident(shape):
    zeros = (0,) * len(shape)
    return pl.BlockSpec(shape, lambda *_: zeros, pipeline_mode=pl.Buffered(1))


def _rms(x, g):
    return x * lax.rsqrt(jnp.mean(x * x, axis=-1, keepdims=True) + EPS) * g


def _dot(a, b):
    return jnp.dot(a, b, preferred_element_type=F32)


def _ada_kernel(c_ref, w_ref, b_ref, o_ref):
    c = c_ref[...]
    s = (c * jax.nn.sigmoid(c)).astype(BF16)
    o_ref[0] = _dot(s, w_ref[0].astype(BF16)) + b_ref[0]


def _ada_call(c_rows, w, b):
    n_layers, d, n = w.shape
    tn = ADA_TILE_N if n % ADA_TILE_N == 0 else n
    return pl.pallas_call(
        _ada_kernel,
        grid=(n_layers, n // tn),
        in_specs=[
            pl.BlockSpec((ADA_ROWS, d), lambda l, j: (0, 0)),
            pl.BlockSpec((1, d, tn), lambda l, j: (l, 0, j)),
            pl.BlockSpec((1, 1, tn), lambda l, j: (l, 0, j)),
        ],
        out_specs=pl.BlockSpec((1, ADA_ROWS, tn), lambda l, j: (l, 0, j)),
        out_shape=jax.ShapeDtypeStruct((n_layers, ADA_ROWS, n), F32),
        compiler_params=_params(("parallel", "parallel")),
        name="ada_proj",
    )(c_rows, w, b.reshape(n_layers, 1, n))


def _gmlp_kernel(x_ref, mod_ref, gn_ref, win_ref, gv_ref, ws_ref, bs_ref, wout_ref, *rest,
                 chunk, emit_v):
    if emit_v:
        xo_ref, vo_ref, z_ref = rest
    else:
        xo_ref, z_ref = rest
    x = x_ref[0]
    mod = mod_ref[0]
    shift, scale, gate = mod[0:1], mod[1:2], mod[2:3]
    h = (_rms(x, gn_ref[...]) * (1.0 + scale) + shift).astype(BF16)
    uv = jax.nn.gelu(_dot(h, win_ref[...]))
    half = uv.shape[1] // 2
    u = uv[:, :half]
    v = _rms(uv[:, half:], gv_ref[...])
    if emit_v:
        vo_ref[0] = v
    vb = v.astype(BF16)
    gdim = half // GMLP_GROUPS
    row = lax.broadcasted_iota(jnp.int32, (chunk, chunk), 0)
    col = lax.broadcasted_iota(jnp.int32, (chunk, chunk), 1)
    causal = row >= col
    for g in range(GMLP_GROUPS):
        w = jnp.where(causal, ws_ref[g], 0.0).astype(BF16)
        bias = bs_ref[:, g:g + 1]
        cs = slice(g * gdim, (g + 1) * gdim)
        for c in range(x.shape[0] // chunk):
            rs = slice(c * chunk, (c + 1) * chunk)
            mixed = _dot(w, vb[rs, cs]) + bias
            z_ref[rs, cs] = (u[rs, cs] * mixed).astype(BF16)
    y = _dot(z_ref[...], wout_ref[...])
    xo_ref[0] = x + gate * y


def _gmlp_call(x, mod, gn, w_in, g_v, w_s, b_s_t, w_out, *, tm, chunk, emit_v):
    b, t, d = x.shape
    u2 = w_in.shape[1]
    half = u2 // 2
    x_spec = pl.BlockSpec((1, tm, d), lambda i, j: (i, j, 0))
    out_specs = [x_spec]
    out_shape = [jax.ShapeDtypeStruct((b, t, d), F32)]
    if emit_v:
        out_specs.append(pl.BlockSpec((1, tm, half), lambda i, j: (i, j, 0)))
        out_shape.append(jax.ShapeDtypeStruct((b, t, half), F32))
    return pl.pallas_call(
        functools.partial(_gmlp_kernel, chunk=chunk, emit_v=emit_v),
        grid=(b, t // tm),
        in_specs=[
            x_spec,
            pl.BlockSpec((1, 8, d), lambda i, j: (i, 0, 0)),
            _resident((1, d)),
            _resident((d, u2)),
            _resident((1, half)),
            _resident((GMLP_GROUPS, chunk, chunk)),
            _resident((chunk, GMLP_GROUPS)),
            _resident((half, d)),
        ],
        out_specs=out_specs,
        out_shape=out_shape,
        scratch_shapes=[pltpu.VMEM((tm, half), BF16)],
        compiler_params=_params(("parallel", "parallel")),
        name="gmlp_layer",
    )(x, mod, gn, w_in, g_v, w_s, b_s_t, w_out)


def _conv_ffn(x, shift, scale, gate, gn, w_up, conv_w, conv_b, w_down, hist_ref, row_ref):
    tm = x.shape[0]
    h = (_rms(x, gn) * (1.0 + scale) + shift).astype(BF16)
    ag = _dot(h, w_up)
    f = ag.shape[1] // 2
    a = ag[:, :f]
    g = ag[:, f:]
    prev1 = hist_ref[1:2, :]
    prev2 = hist_ref[0:1, :]
    row = lax.broadcasted_iota(jnp.int32, (tm, 1), 0)
    g1 = jnp.where(row == 0, prev1, pltpu.roll(g, 1, axis=0))
    g2 = jnp.where(row == 0, prev2, jnp.where(row == 1, prev1, pltpu.roll(g, 2, axis=0)))
    gc = conv_b + conv_w[0:1] * g2 + conv_w[1:2] * g1 + conv_w[2:3] * g
    last = g[tm - (CONV_W - 1):, :]
    hist_ref[...] = last
    row_ref[0] = last
    y = _dot((gc * jax.nn.sigmoid(gc) * a).astype(BF16), w_down)
    return x + gate * y


def _ffn0_kernel(x_ref, mod_ref, hist_in_ref, gn_ref, wup_ref, cw_ref, cb_ref, wdown_ref,
                 gkv_ref, wkv_ref, gq_ref, wq_ref,
                 xo_ref, row_ref, k_ref, v_ref, kb_ref, vb_ref, qb_ref, hist_ref):
    @pl.when(pl.program_id(1) == 0)
    def _():
        hist_ref[...] = hist_in_ref[0]

    mod = mod_ref[0, 0]
    mod_next = mod_ref[1, 0]
    x = _conv_ffn(x_ref[0], mod[3:4], mod[4:5], mod[5:6], gn_ref[...], wup_ref[...], cw_ref[...],
                  cb_ref[...], wdown_ref[...], hist_ref, row_ref)
    xo_ref[0] = x
    hk = (_rms(x, gkv_ref[...]) * (1.0 + mod[7:8]) + mod[6:7]).astype(BF16)
    kv = _dot(hk, wkv_ref[...])
    hd = kv.shape[1] // 2
    k_ref[0] = kv[:, :hd]
    v_ref[0] = kv[:, hd:]
    kb_ref[0] = kv[:, :hd].astype(BF16)
    vb_ref[0] = kv[:, hd:].astype(BF16)
    hq = (_rms(x, gq_ref[...]) * (1.0 + mod_next[1:2]) + mod_next[0:1]).astype(BF16)
    qb_ref[0] = (_dot(hq, wq_ref[...]) * (SB_HEAD_DIM ** -0.5)).astype(BF16)


def _ffn1_kernel(x_ref, o_ref, mod_ref, hist_in_ref, wo_ref, gn_ref, wup_ref, cw_ref, cb_ref,
                 wdown_ref, gfin_ref, y_ref, row_ref, hist_ref):
    @pl.when(pl.program_id(1) == 0)
    def _():
        hist_ref[...] = hist_in_ref[0]

    mod = mod_ref[0]
    x = x_ref[0] + mod[2:3] * _dot(o_ref[0], wo_ref[...])
    x = _conv_ffn(x, mod[3:4], mod[4:5], mod[5:6], gn_ref[...], wup_ref[...], cw_ref[...],
                  cb_ref[...], wdown_ref[...], hist_ref, row_ref)
    y_ref[0] = _rms(x, gfin_ref[...])


def _ffn0_call(x, mod01, hist, gn, w_up, conv_w, conv_b, w_down, g_kv, w_kv, g_q, w_q, *, tm):
    b, t, d = x.shape
    f = w_down.shape[0]
    hd = w_q.shape[1]
    row_block = lambda last, dt: (pl.BlockSpec((1, tm, last), lambda i, j: (i, j, 0)),
                                  jax.ShapeDtypeStruct((b, t, last), dt))
    outs = [row_block(d, F32),
            (pl.BlockSpec((1, CONV_W - 1, f), lambda i, j: (i, 0, 0)),
             jax.ShapeDtypeStruct((b, CONV_W - 1, f), F32)),
            row_block(hd, F32), row_block(hd, F32),
            row_block(hd, BF16), row_block(hd, BF16), row_block(hd, BF16)]
    return pl.pallas_call(
        _ffn0_kernel,
        grid=(b, t // tm),
        in_specs=[
            pl.BlockSpec((1, tm, d), lambda i, j: (i, j, 0)),
            pl.BlockSpec((2, 1, 8, d), lambda i, j: (0, i, 0, 0)),
            pl.BlockSpec((1, CONV_W - 1, f), lambda i, j: (i, 0, 0)),
            _resident((1, d)), _resident((d, 2 * f)), _resident((CONV_W, f)), _resident((1, f)),
            _resident((f, d)), _resident((1, d)), _resident((d, 2 * hd)), _resident((1, d)),
            _resident((d, hd)),
        ],
        out_specs=[o[0] for o in outs],
        out_shape=[o[1] for o in outs],
        scratch_shapes=[pltpu.VMEM((CONV_W - 1, f), F32)],
        compiler_params=_params(("parallel", "arbitrary")),
        name="ffn0_kv_q",
    )(x, mod01, hist, gn, w_up, conv_w, conv_b, w_down, g_kv, w_kv, g_q, w_q)


def _ffn1_call(x, o, mod, hist, w_o, gn, w_up, conv_w, conv_b, w_down, g_fin, *, tm):
    b, t, d = x.shape
    f = w_down.shape[0]
    hd = w_o.shape[0]
    return pl.pallas_call(
        _ffn1_kernel,
        grid=(b, t // tm),
        in_specs=[
            pl.BlockSpec((1, tm, d), lambda i, j: (i, j, 0)),
            pl.BlockSpec((1, tm, hd), lambda i, j: (i, j, 0)),
            pl.BlockSpec((1, 8, d), lambda i, j: (i, 0, 0)),
            pl.BlockSpec((1, CONV_W - 1, f), lambda i, j: (i, 0, 0)),
            _resident((hd, d)), _resident((1, d)), _resident((d, 2 * f)), _resident((CONV_W, f)),
            _resident((1, f)), _resident((f, d)), _resident((1, d)),
        ],
        out_specs=[pl.BlockSpec((1, tm, d), lambda i, j: (i, j, 0)),
                   pl.BlockSpec((1, CONV_W - 1, f), lambda i, j: (i, 0, 0))],
        out_shape=[jax.ShapeDtypeStruct((b, t, d), F32),
                   jax.ShapeDtypeStruct((b, CONV_W - 1, f), F32)],
        scratch_shapes=[pltpu.VMEM((CONV_W - 1, f), F32)],
        compiler_params=_params(("parallel", "arbitrary")),
        name="ffn1_final",
    )(x, o, mod, hist, w_o, gn, w_up, conv_w, conv_b, w_down, g_fin)


def _attn_kernel(q_ref, k_ref, v_ref, sfx_ref, o_ref, *, tq, q_pos0):
    tk, win, sfx_n = ATTN_BLOCK, ATTN_WINDOW, ATTN_SUFFIX
    groups = ATTN_LANES // LANES
    q_start = q_pos0 + pl.program_id(2) * tq
    win_end = (q_start // tk + 1) * tk
    win_start = jnp.maximum(win_end - win, 0)
    lane = lax.broadcasted_iota(jnp.int32, (1, LANES), 1)
    head0 = lane < SB_HEAD_DIM
    t_iota = lax.broadcasted_iota(jnp.int32, (tq, 1), 0)
    t_pos = q_start + jnp.concatenate([t_iota, t_iota], axis=0)

    def stacked_q(g):
        q = q_ref[0, :, g * LANES:(g + 1) * LANES]
        zero = jnp.zeros_like(q)
        return jnp.concatenate([jnp.where(head0, q, zero), jnp.where(head0, zero, q)], axis=0)

    q2s = [stacked_q(g) for g in range(groups)]

    def suffix_sums(nl):
        width = nl.shape[1]
        hi = nl.astype(BF16)
        lo = (nl - hi.astype(F32)).astype(BF16)
        parts = jnp.concatenate([hi, lo], axis=0)
        if width > sfx_n:
            cut = width - sfx_n
            new = _dot(parts[:, cut:], sfx_ref[...])
            old = _dot(parts[:, :cut], sfx_ref[0:cut, 0:cut]) + new[:, 0:1]
            sums = jnp.concatenate([old, new], axis=1)
        else:
            sums = _dot(parts, sfx_ref[0:width, 0:width])
        return sums[:2 * tq] + sums[2 * tq:]

    def block(g, start, width, o, nacc, masked):
        k_blk = k_ref[0, pl.ds(start, width), g * LANES:(g + 1) * LANES]
        v_blk = v_ref[0, pl.ds(start, width), g * LANES:(g + 1) * LANES]
        z = lax.dot_general(q2s[g], k_blk, (((1,), (1,)), ((), ())), preferred_element_type=F32)
        nl = jnp.maximum(z, 0.0) + jnp.log(1.0 + jnp.exp(-jnp.abs(z)))
        if masked:
            mask = (start + lax.broadcasted_iota(jnp.int32, (1, width), 1)) < t_pos
            nl = jnp.where(mask, nl, 0.0)
        incl = suffix_sums(nl)
        a = jnp.exp(z - incl - nacc)
        if masked:
            a = jnp.where(mask, a, 0.0)
        pv = _dot(a.astype(BF16), v_blk)
        o = o + jnp.where(head0, pv[:tq], pv[tq:])
        return o, nacc + incl[:, 0:1]

    def alive(naccs):
        low = functools.reduce(jnp.minimum, naccs)
        return (jnp.min(low) < ATTN_DEAD_NEG_LOG).astype(jnp.int32)

    first = [block(g, pl.multiple_of(win_start, tk), win, jnp.zeros((tq, LANES), F32),
                   jnp.zeros((2 * tq, 1), F32), True) for g in range(groups)]
    os, naccs = [f[0] for f in first], [f[1] for f in first]

    def cond(carry):
        return jnp.logical_and(carry[0] >= 0, carry[1] > 0)

    def body(carry):
        kb, _, os, naccs = carry
        nxt = [block(g, pl.multiple_of(kb * tk, tk), tk, os[g], naccs[g], False) for g in range(groups)]
        os, naccs = [n[0] for n in nxt], [n[1] for n in nxt]
        return kb - 1, alive(naccs), os, naccs

    carry = lax.while_loop(cond, body, (win_start // tk - 1, alive(naccs), os, naccs))
    o_ref[0] = jnp.concatenate(carry[2], axis=1).astype(o_ref.dtype)


def _attn_call(q, k, v, *, tq, q_pos0):
    b, t, hd = q.shape
    s = k.shape[1]
    assert ATTN_BLOCK % tq == 0 and q_pos0 % tq == 0 and s % ATTN_BLOCK == 0 and s >= ATTN_WINDOW
    assert hd % ATTN_LANES == 0 and ATTN_WINDOW - ATTN_SUFFIX <= ATTN_SUFFIX
    idx = jnp.arange(ATTN_SUFFIX)
    suffix = (idx[:, None] >= idx[None, :]).astype(BF16)
    kv_spec = pl.BlockSpec((1, s, ATTN_LANES), lambda i, h, j: (i, 0, h))
    q_spec = pl.BlockSpec((1, tq, ATTN_LANES), lambda i, h, j: (i, j, h))
    return pl.pallas_call(
        functools.partial(_attn_kernel, tq=tq, q_pos0=q_pos0),
        grid=(b, hd // ATTN_LANES, t // tq),
        in_specs=[q_spec, kv_spec, kv_spec, _resident((ATTN_SUFFIX, ATTN_SUFFIX))],
        out_specs=q_spec,
        out_shape=jax.ShapeDtypeStruct((b, t, hd), BF16),
        compiler_params=_params(("parallel", "parallel", "parallel")),
        name="sb_attention",
    )(q, k, v, suffix)


def _forward(x, mods, conv_hist, cache_k, cache_v, wts, *, tm, chunk, emit_v):
    b, t, d = x.shape
    x1 = _gmlp_call(x, mods[0], wts["norm_mix0"], wts["gmlp_w_in"], wts["gmlp_norm_v"],
                    wts["gmlp_w_s"][:, :chunk, :chunk], wts["gmlp_b_s"][:, :chunk].T,
                    wts["gmlp_w_out"], tm=tm, chunk=chunk, emit_v=emit_v)
    v_rows = None
    if emit_v:
        x1, v_rows = x1
    else:
        (x1,) = x1
    x2, conv0, k, v, kb, vb, qb = _ffn0_call(
        x1, mods, conv_hist[0], wts["norm_ffn0"], wts["ffn_w_up0"], wts["ffn_conv_w"][0],
        wts["ffn_conv_b0"], wts["ffn_w_down0"], wts["kv_norm"], wts["w_kv"], wts["norm_mix1"],
        wts["sb_w_q"], tm=tm)
    if cache_k is None:
        o = _attn_call(qb, kb, vb, tq=ATTN_BLOCK, q_pos0=0)
    else:
        past = cache_k.shape[1]
        pad = -(past + t) % ATTN_BLOCK
        def all_keys(cache, new):
            rows = jnp.concatenate([cache.reshape(b, past, -1).astype(BF16), new], axis=1)
            return jnp.pad(rows, ((0, 0), (0, pad), (0, 0)))
        o = _attn_call(qb, all_keys(cache_k, kb), all_keys(cache_v, vb), tq=t, q_pos0=past)
    y, conv1 = _ffn1_call(x2, o, mods[1], conv_hist[1], wts["sb_w_o"], wts["norm_ffn1"],
                          wts["ffn_w_up1"], wts["ffn_conv_w"][1], wts["ffn_conv_b1"],
                          wts["ffn_w_down1"], wts["final_norm"], tm=tm)
    heads = (b, t, SB_HEADS, SB_HEAD_DIM)
    return y, k.reshape(heads), v.reshape(heads), jnp.stack([conv0, conv1]), v_rows


def kernel(x_prompt, x_sample, c_prompt, c_sample, cache_k, cache_v, state_conv, ada_w, ada_b, norm_mix, norm_ffn, gmlp_w_in, gmlp_norm_v, gmlp_w_s, gmlp_b_s, gmlp_w_out, kv_ada_w, kv_ada_b, kv_norm, w_kv, sb_w_q, sb_w_o, ffn_w_up, ffn_conv_w, ffn_conv_b, ffn_w_down, final_norm):
    depth, d = norm_mix.shape
    assert depth == 2 and gmlp_w_in.shape[0] == 1 and sb_w_q.shape[0] == 1
    bp, tp, _ = x_prompt.shape
    bs, ts, _ = x_sample.shape
    assert bp + bs <= ADA_ROWS

    c_rows = jnp.zeros((ADA_ROWS, d), F32).at[:bp].set(c_prompt).at[bp:bp + bs].set(c_sample)
    ada = _ada_call(c_rows, ada_w, ada_b)
    kv_ada = _ada_call(c_rows, kv_ada_w[None], kv_ada_b[None])
    ada = ada.reshape(depth, ADA_ROWS, N_ADA, d)
    kv_rows = jnp.broadcast_to(kv_ada.reshape(1, ADA_ROWS, 2, d), (depth, ADA_ROWS, 2, d))
    mods = jnp.concatenate([ada, kv_rows], axis=2)

    row = lambda a: a.reshape(1, -1)
    wts = {
        "norm_mix0": row(norm_mix[0]), "norm_mix1": row(norm_mix[1]),
        "norm_ffn0": row(norm_ffn[0]), "norm_ffn1": row(norm_ffn[1]),
        "gmlp_w_in": gmlp_w_in[0].astype(BF16), "gmlp_norm_v": row(gmlp_norm_v[0]),
        "gmlp_w_s": gmlp_w_s[0], "gmlp_b_s": gmlp_b_s[0], "gmlp_w_out": gmlp_w_out[0].astype(BF16),
        "kv_norm": row(kv_norm), "w_kv": w_kv.astype(BF16),
        "sb_w_q": sb_w_q[0].astype(BF16), "sb_w_o": sb_w_o[0].astype(BF16),
        "ffn_w_up0": ffn_w_up[0].astype(BF16), "ffn_w_up1": ffn_w_up[1].astype(BF16),
        "ffn_conv_w": ffn_conv_w, "ffn_conv_b0": row(ffn_conv_b[0]), "ffn_conv_b1": row(ffn_conv_b[1]),
        "ffn_w_down0": ffn_w_down[0].astype(BF16), "ffn_w_down1": ffn_w_down[1].astype(BF16),
        "final_norm": row(final_norm),
    }

    conv_zero = jnp.zeros((depth, bp, CONV_W - 1, state_conv.shape[-1]), F32)
    tm_prompt = min(tp, 256)
    y_p, k_p, v_p, conv_p, _ = _forward(
        x_prompt, mods[:, :bp], conv_zero, None, None, wts,
        tm=tm_prompt, chunk=min(tp, GMLP_CHUNK), emit_v=False)
    y_s, k_s, v_s, conv_s, gv_s = _forward(
        x_sample, mods[:, bp:bp + bs], state_conv, cache_k, cache_v, wts,
        tm=ts, chunk=min(ts, GMLP_CHUNK), emit_v=True)
    return (y_p, y_s, k_p, v_p, k_s, v_s, conv_p, conv_s, gv_s[None])
```

```python
import functools

import jax
import jax.numpy as jnp
from jax import lax
from jax.experimental import pallas as pl
from jax.experimental.pallas import tpu as pltpu

F32 = jnp.float32
BF16 = jnp.bfloat16

EPS = 1e-6
GMLP_CHUNK = 128
GMLP_GROUPS = 8
SB_HEADS = 16
SB_HEAD_DIM = 64
CONV_W = 3
N_ADA = 6

LANES = 128
ADA_ROWS = 16
ADA_TILE_N = 1536
ATTN_BLOCK = 128
ATTN_WINDOW = 3 * ATTN_BLOCK
ATTN_SUFFIX = 2 * ATTN_BLOCK
ATTN_LANES = 2 * LANES
LOG2E = 1.4426950408889634
ATTN_DEAD_NEG_LOG2 = 110.0 * LOG2E
ATTN_MASKED_LOGIT = -1e30
VMEM_LIMIT_BYTES = 56 * 1024 * 1024


def _params(semantics):
    return pltpu.CompilerParams(dimension_semantics=semantics, vmem_limit_bytes=VMEM_LIMIT_BYTES)


def _resident(shape):
    zeros = (0,) * len(shape)
    return pl.BlockSpec(shape, lambda *_: zeros, pipeline_mode=pl.Buffered(1))


def _rms(x, g):
    return x * lax.rsqrt(jnp.mean(x * x, axis=-1, keepdims=True) + EPS) * g


def _dot(a, b):
    return jnp.dot(a, b, preferred_element_type=F32)


def _ada_kernel(c_ref, w_ref, b_ref, o_ref):
    c = c_ref[...]
    s = (c * jax.nn.sigmoid(c)).astype(BF16)
    o_ref[0] = _dot(s, w_ref[0].astype(BF16)) + b_ref[0]


def _ada_call(c_rows, w, b):
    n_layers, d, n = w.shape
    tn = ADA_TILE_N if n % ADA_TILE_N == 0 else n
    return pl.pallas_call(
        _ada_kernel,
        grid=(n_layers, n // tn),
        in_specs=[
            pl.BlockSpec((ADA_ROWS, d), lambda l, j: (0, 0)),
            pl.BlockSpec((1, d, tn), lambda l, j: (l, 0, j)),
            pl.BlockSpec((1, 1, tn), lambda l, j: (l, 0, j)),
        ],
        out_specs=pl.BlockSpec((1, ADA_ROWS, tn), lambda l, j: (l, 0, j)),
        out_shape=jax.ShapeDtypeStruct((n_layers, ADA_ROWS, n), F32),
        compiler_params=_params(("parallel", "parallel")),
        name="ada_proj",
    )(c_rows, w, b.reshape(n_layers, 1, n))


def _gmlp_kernel(x_ref, mod_ref, gn_ref, win_ref, gv_ref, ws_ref, bs_ref, wout_ref, *rest,
                 chunk, emit_v):
    if emit_v:
        xo_ref, vo_ref, z_ref = rest
    else:
        xo_ref, z_ref = rest
    x = x_ref[0]
    mod = mod_ref[0]
    shift, scale, gate = mod[0:1], mod[1:2], mod[2:3]
    h = (_rms(x, gn_ref[...]) * (1.0 + scale) + shift).astype(BF16)
    uv = jax.nn.gelu(_dot(h, win_ref[...]))
    half = uv.shape[1] // 2
    u = uv[:, :half]
    v = _rms(uv[:, half:], gv_ref[...])
    if emit_v:
        vo_ref[0] = v
    vb = v.astype(BF16)
    gdim = half // GMLP_GROUPS
    row = lax.broadcasted_iota(jnp.int32, (chunk, chunk), 0)
    col = lax.broadcasted_iota(jnp.int32, (chunk, chunk), 1)
    causal = row >= col
    for g in range(GMLP_GROUPS):
        w = jnp.where(causal, ws_ref[g], 0.0).astype(BF16)
        bias = bs_ref[:, g:g + 1]
        cs = slice(g * gdim, (g + 1) * gdim)
        for c in range(x.shape[0] // chunk):
            rs = slice(c * chunk, (c + 1) * chunk)
            mixed = _dot(w, vb[rs, cs]) + bias
            z_ref[rs, cs] = (u[rs, cs] * mixed).astype(BF16)
    y = _dot(z_ref[...], wout_ref[...])
    xo_ref[0] = x + gate * y


def _gmlp_call(x, mod, gn, w_in, g_v, w_s, b_s_t, w_out, *, tm, chunk, emit_v):
    b, t, d = x.shape
    u2 = w_in.shape[1]
    half = u2 // 2
    x_spec = pl.BlockSpec((1, tm, d), lambda i, j: (i, j, 0))
    out_specs = [x_spec]
    out_shape = [jax.ShapeDtypeStruct((b, t, d), F32)]
    if emit_v:
        out_specs.append(pl.BlockSpec((1, tm, half), lambda i, j: (i, j, 0)))
        out_shape.append(jax.ShapeDtypeStruct((b, t, half), F32))
    return pl.pallas_call(
        functools.partial(_gmlp_kernel, chunk=chunk, emit_v=emit_v),
        grid=(b, t // tm),
        in_specs=[
            x_spec,
            pl.BlockSpec((1, 8, d), lambda i, j: (i, 0, 0)),
            _resident((1, d)),
            _resident((d, u2)),
            _resident((1, half)),
            _resident((GMLP_GROUPS, chunk, chunk)),
            _resident((chunk, GMLP_GROUPS)),
            _resident((half, d)),
        ],
        out_specs=out_specs,
        out_shape=out_shape,
        scratch_shapes=[pltpu.VMEM((tm, half), BF16)],
        compiler_params=_params(("parallel", "parallel")),
        name="gmlp_layer",
    )(x, mod, gn, w_in, g_v, w_s, b_s_t, w_out)


def _conv_ffn(x, shift, scale, gate, gn, w_up, conv_w, conv_b, w_down, hist_ref, row_ref):
    tm = x.shape[0]
    h = (_rms(x, gn) * (1.0 + scale) + shift).astype(BF16)
    ag = _dot(h, w_up)
    f = ag.shape[1] // 2
    a = ag[:, :f]
    g = ag[:, f:]
    prev1 = hist_ref[1:2, :]
    prev2 = hist_ref[0:1, :]
    row = lax.broadcasted_iota(jnp.int32, (tm, 1), 0)
    g1 = jnp.where(row == 0, prev1, pltpu.roll(g, 1, axis=0))
    g2 = jnp.where(row == 0, prev2, jnp.where(row == 1, prev1, pltpu.roll(g, 2, axis=0)))
    gc = conv_b + conv_w[0:1] * g2 + conv_w[1:2] * g1 + conv_w[2:3] * g
    last = g[tm - (CONV_W - 1):, :]
    hist_ref[...] = last
    row_ref[0] = last
    y = _dot((gc * jax.nn.sigmoid(gc) * a).astype(BF16), w_down)
    return x + gate * y


def _ffn0_kernel(x_ref, mod_ref, hist_in_ref, gn_ref, wup_ref, cw_ref, cb_ref, wdown_ref,
                 gkv_ref, wkv_ref, gq_ref, wq_ref,
                 xo_ref, row_ref, k_ref, v_ref, kb_ref, vb_ref, qb_ref, hist_ref):
    @pl.when(pl.program_id(1) == 0)
    def _():
        hist_ref[...] = hist_in_ref[0]

    mod = mod_ref[0, 0]
    mod_next = mod_ref[1, 0]
    x = _conv_ffn(x_ref[0], mod[3:4], mod[4:5], mod[5:6], gn_ref[...], wup_ref[...], cw_ref[...],
                  cb_ref[...], wdown_ref[...], hist_ref, row_ref)
    xo_ref[0] = x
    hk = (_rms(x, gkv_ref[...]) * (1.0 + mod[7:8]) + mod[6:7]).astype(BF16)
    kv = _dot(hk, wkv_ref[...])
    hd = kv.shape[1] // 2
    k_ref[0] = kv[:, :hd]
    v_ref[0] = kv[:, hd:]
    kb_ref[0] = kv[:, :hd].astype(BF16)
    vb_ref[0] = kv[:, hd:].astype(BF16)
    hq = (_rms(x, gq_ref[...]) * (1.0 + mod_next[1:2]) + mod_next[0:1]).astype(BF16)
    qb_ref[0] = (_dot(hq, wq_ref[...]) * (SB_HEAD_DIM ** -0.5 * LOG2E)).astype(BF16)


def _ffn1_kernel(x_ref, o_ref, mod_ref, hist_in_ref, wo_ref, gn_ref, wup_ref, cw_ref, cb_ref,
                 wdown_ref, gfin_ref, y_ref, row_ref, hist_ref):
    @pl.when(pl.program_id(1) == 0)
    def _():
        hist_ref[...] = hist_in_ref[0]

    mod = mod_ref[0]
    x = x_ref[0] + mod[2:3] * _dot(o_ref[0], wo_ref[...])
    x = _conv_ffn(x, mod[3:4], mod[4:5], mod[5:6], gn_ref[...], wup_ref[...], cw_ref[...],
                  cb_ref[...], wdown_ref[...], hist_ref, row_ref)
    y_ref[0] = _rms(x, gfin_ref[...])


def _ffn0_call(x, mod01, hist, gn, w_up, conv_w, conv_b, w_down, g_kv, w_kv, g_q, w_q, *, tm):
    b, t, d = x.shape
    f = w_down.shape[0]
    hd = w_q.shape[1]
    row_block = lambda last, dt: (pl.BlockSpec((1, tm, last), lambda i, j: (i, j, 0)),
                                  jax.ShapeDtypeStruct((b, t, last), dt))
    outs = [row_block(d, F32),
            (pl.BlockSpec((1, CONV_W - 1, f), lambda i, j: (i, 0, 0)),
             jax.ShapeDtypeStruct((b, CONV_W - 1, f), F32)),
            row_block(hd, F32), row_block(hd, F32),
            row_block(hd, BF16), row_block(hd, BF16), row_block(hd, BF16)]
    return pl.pallas_call(
        _ffn0_kernel,
        grid=(b, t // tm),
        in_specs=[
            pl.BlockSpec((1, tm, d), lambda i, j: (i, j, 0)),
            pl.BlockSpec((2, 1, 8, d), lambda i, j: (0, i, 0, 0)),
            pl.BlockSpec((1, CONV_W - 1, f), lambda i, j: (i, 0, 0)),
            _resident((1, d)), _resident((d, 2 * f)), _resident((CONV_W, f)), _resident((1, f)),
            _resident((f, d)), _resident((1, d)), _resident((d, 2 * hd)), _resident((1, d)),
            _resident((d, hd)),
        ],
        out_specs=[o[0] for o in outs],
        out_shape=[o[1] for o in outs],
        scratch_shapes=[pltpu.VMEM((CONV_W - 1, f), F32)],
        compiler_params=_params(("parallel", "arbitrary")),
        name="ffn0_kv_q",
    )(x, mod01, hist, gn, w_up, conv_w, conv_b, w_down, g_kv, w_kv, g_q, w_q)


def _ffn1_call(x, o, mod, hist, w_o, gn, w_up, conv_w, conv_b, w_down, g_fin, *, tm):
    b, t, d = x.shape
    f = w_down.shape[0]
    hd = w_o.shape[0]
    return pl.pallas_call(
        _ffn1_kernel,
        grid=(b, t // tm),
        in_specs=[
            pl.BlockSpec((1, tm, d), lambda i, j: (i, j, 0)),
            pl.BlockSpec((1, tm, hd), lambda i, j: (i, j, 0)),
            pl.BlockSpec((1, 8, d), lambda i, j: (i, 0, 0)),
            pl.BlockSpec((1, CONV_W - 1, f), lambda i, j: (i, 0, 0)),
            _resident((hd, d)), _resident((1, d)), _resident((d, 2 * f)), _resident((CONV_W, f)),
            _resident((1, f)), _resident((f, d)), _resident((1, d)),
        ],
        out_specs=[pl.BlockSpec((1, tm, d), lambda i, j: (i, j, 0)),
                   pl.BlockSpec((1, CONV_W - 1, f), lambda i, j: (i, 0, 0))],
        out_shape=[jax.ShapeDtypeStruct((b, t, d), F32),
                   jax.ShapeDtypeStruct((b, CONV_W - 1, f), F32)],
        scratch_shapes=[pltpu.VMEM((CONV_W - 1, f), F32)],
        compiler_params=_params(("parallel", "arbitrary")),
        name="ffn1_final",
    )(x, o, mod, hist, w_o, gn, w_up, conv_w, conv_b, w_down, g_fin)


def _attn_kernel(q_ref, k_ref, v_ref, sfx_ref, o_ref, *, tq, q_pos0):
    tk, win, sfx_n = ATTN_BLOCK, ATTN_WINDOW, ATTN_SUFFIX
    groups = ATTN_LANES // LANES
    q_start = q_pos0 + pl.program_id(2) * tq
    win_end = (q_start // tk + 1) * tk
    win_start = jnp.maximum(win_end - win, 0)
    lane = lax.broadcasted_iota(jnp.int32, (1, LANES), 1)
    head0 = lane < SB_HEAD_DIM
    t_iota = lax.broadcasted_iota(jnp.int32, (tq, 1), 0)
    t_pos = q_start + jnp.concatenate([t_iota, t_iota], axis=0)

    def stacked_q(g):
        q = q_ref[0, :, g * LANES:(g + 1) * LANES]
        zero = jnp.zeros_like(q)
        return jnp.concatenate([jnp.where(head0, q, zero), jnp.where(head0, zero, q)], axis=0)

    q2s = [stacked_q(g) for g in range(groups)]

    def suffix_sums(nl):
        width = nl.shape[1]
        hi = nl.astype(BF16)
        lo = (nl - hi.astype(F32)).astype(BF16)
        parts = jnp.concatenate([hi, lo], axis=0)
        if width > sfx_n:
            cut = width - sfx_n
            new = _dot(parts[:, cut:], sfx_ref[...])
            old = _dot(parts[:, :cut], sfx_ref[0:cut, 0:cut]) + new[:, 0:1]
            sums = jnp.concatenate([old, new], axis=1)
        else:
            sums = _dot(parts, sfx_ref[0:width, 0:width])
        return sums[:2 * tq] + sums[2 * tq:]

    def block(g, start, width, o, nacc, masked):
        k_blk = k_ref[0, pl.ds(start, width), g * LANES:(g + 1) * LANES]
        v_blk = v_ref[0, pl.ds(start, width), g * LANES:(g + 1) * LANES]
        z = lax.dot_general(q2s[g], k_blk, (((1,), (1,)), ((), ())), preferred_element_type=F32)
        if masked:
            mask = (start + lax.broadcasted_iota(jnp.int32, (1, width), 1)) < t_pos
            z = jnp.where(mask, z, ATTN_MASKED_LOGIT)
        nl = jnp.maximum(z, 0.0) + jnp.log(1.0 + jnp.exp2(-jnp.abs(z))) * LOG2E
        incl = suffix_sums(nl)
        a = jnp.exp2(z - incl - nacc)
        pv = _dot(a.astype(BF16), v_blk)
        o = o + jnp.where(head0, pv[:tq], pv[tq:])
        return o, nacc + incl[:, 0:1]

    def alive(naccs):
        low = functools.reduce(jnp.minimum, naccs)
        return (jnp.min(low) < ATTN_DEAD_NEG_LOG2).astype(jnp.int32)

    first = [block(g, pl.multiple_of(win_start, tk), win, jnp.zeros((tq, LANES), F32),
                   jnp.zeros((2 * tq, 1), F32), True) for g in range(groups)]
    os, naccs = [f[0] for f in first], [f[1] for f in first]

    def cond(carry):
        return jnp.logical_and(carry[0] >= 0, carry[1] > 0)

    def body(carry):
        kb, _, os, naccs = carry
        nxt = [block(g, pl.multiple_of(kb * tk, tk), tk, os[g], naccs[g], False) for g in range(groups)]
        os, naccs = [n[0] for n in nxt], [n[1] for n in nxt]
        return kb - 1, alive(naccs), os, naccs

    carry = lax.while_loop(cond, body, (win_start // tk - 1, alive(naccs), os, naccs))
    o_ref[0] = jnp.concatenate(carry[2], axis=1).astype(o_ref.dtype)


def _attn_call(q, k, v, *, tq, q_pos0):
    b, t, hd = q.shape
    s = k.shape[1]
    assert ATTN_BLOCK % tq == 0 and q_pos0 % tq == 0 and s % ATTN_BLOCK == 0 and s >= ATTN_WINDOW
    assert hd % ATTN_LANES == 0 and ATTN_WINDOW - ATTN_SUFFIX <= ATTN_SUFFIX
    idx = jnp.arange(ATTN_SUFFIX)
    suffix = (idx[:, None] >= idx[None, :]).astype(BF16)
    kv_spec = pl.BlockSpec((1, s, ATTN_LANES), lambda i, h, j: (i, 0, h))
    q_spec = pl.BlockSpec((1, tq, ATTN_LANES), lambda i, h, j: (i, j, h))
    return pl.pallas_call(
        functools.partial(_attn_kernel, tq=tq, q_pos0=q_pos0),
        grid=(b, hd // ATTN_LANES, t // tq),
        in_specs=[q_spec, kv_spec, kv_spec, _resident((ATTN_SUFFIX, ATTN_SUFFIX))],
        out_specs=q_spec,
        out_shape=jax.ShapeDtypeStruct((b, t, hd), BF16),
        compiler_params=_params(("parallel", "parallel", "parallel")),
        name="sb_attention",
    )(q, k, v, suffix)


def _forward(x, mods, conv_hist, cache_k, cache_v, wts, *, tm, chunk, emit_v):
    b, t, d = x.shape
    x1 = _gmlp_call(x, mods[0], wts["norm_mix0"], wts["gmlp_w_in"], wts["gmlp_norm_v"],
                    wts["gmlp_w_s"][:, :chunk, :chunk], wts["gmlp_b_s"][:, :chunk].T,
                    wts["gmlp_w_out"], tm=tm, chunk=chunk, emit_v=emit_v)
    v_rows = None
    if emit_v:
        x1, v_rows = x1
    else:
        (x1,) = x1
    x2, conv0, k, v, kb, vb, qb = _ffn0_call(
        x1, mods, conv_hist[0], wts["norm_ffn0"], wts["ffn_w_up0"], wts["ffn_conv_w"][0],
        wts["ffn_conv_b0"], wts["ffn_w_down0"], wts["kv_norm"], wts["w_kv"], wts["norm_mix1"],
        wts["sb_w_q"], tm=tm)
    if cache_k is None:
        o = _attn_call(qb, kb, vb, tq=ATTN_BLOCK, q_pos0=0)
    else:
        past = cache_k.shape[1]
        pad = -(past + t) % ATTN_BLOCK
        def all_keys(cache, new):
            rows = jnp.concatenate([cache.reshape(b, past, -1).astype(BF16), new], axis=1)
            return jnp.pad(rows, ((0, 0), (0, pad), (0, 0)))
        o = _attn_call(qb, all_keys(cache_k, kb), all_keys(cache_v, vb), tq=t, q_pos0=past)
    y, conv1 = _ffn1_call(x2, o, mods[1], conv_hist[1], wts["sb_w_o"], wts["norm_ffn1"],
                          wts["ffn_w_up1"], wts["ffn_conv_w"][1], wts["ffn_conv_b1"],
                          wts["ffn_w_down1"], wts["final_norm"], tm=tm)
    heads = (b, t, SB_HEADS, SB_HEAD_DIM)
    return y, k.reshape(heads), v.reshape(heads), jnp.stack([conv0, conv1]), v_rows


def kernel(x_prompt, x_sample, c_prompt, c_sample, cache_k, cache_v, state_conv, ada_w, ada_b, norm_mix, norm_ffn, gmlp_w_in, gmlp_norm_v, gmlp_w_s, gmlp_b_s, gmlp_w_out, kv_ada_w, kv_ada_b, kv_norm, w_kv, sb_w_q, sb_w_o, ffn_w_up, ffn_conv_w, ffn_conv_b, ffn_w_down, final_norm):
    depth, d = norm_mix.shape
    assert depth == 2 and gmlp_w_in.shape[0] == 1 and sb_w_q.shape[0] == 1
    bp, tp, _ = x_prompt.shape
    bs, ts, _ = x_sample.shape
    assert bp + bs <= ADA_ROWS

    c_rows = jnp.zeros((ADA_ROWS, d), F32).at[:bp].set(c_prompt).at[bp:bp + bs].set(c_sample)
    ada = _ada_call(c_rows, ada_w, ada_b)
    kv_ada = _ada_call(c_rows, kv_ada_w[None], kv_ada_b[None])
    ada = ada.reshape(depth, ADA_ROWS, N_ADA, d)
    kv_rows = jnp.broadcast_to(kv_ada.reshape(1, ADA_ROWS, 2, d), (depth, ADA_ROWS, 2, d))
    mods = jnp.concatenate([ada, kv_rows], axis=2)

    row = lambda a: a.reshape(1, -1)
    wts = {
        "norm_mix0": row(norm_mix[0]), "norm_mix1": row(norm_mix[1]),
        "norm_ffn0": row(norm_ffn[0]), "norm_ffn1": row(norm_ffn[1]),
        "gmlp_w_in": gmlp_w_in[0].astype(BF16), "gmlp_norm_v": row(gmlp_norm_v[0]),
        "gmlp_w_s": gmlp_w_s[0], "gmlp_b_s": gmlp_b_s[0], "gmlp_w_out": gmlp_w_out[0].astype(BF16),
        "kv_norm": row(kv_norm), "w_kv": w_kv.astype(BF16),
        "sb_w_q": sb_w_q[0].astype(BF16), "sb_w_o": sb_w_o[0].astype(BF16),
        "ffn_w_up0": ffn_w_up[0].astype(BF16), "ffn_w_up1": ffn_w_up[1].astype(BF16),
        "ffn_conv_w": ffn_conv_w, "ffn_conv_b0": row(ffn_conv_b[0]), "ffn_conv_b1": row(ffn_conv_b[1]),
        "ffn_w_down0": ffn_w_down[0].astype(BF16), "ffn_w_down1": ffn_w_down[1].astype(BF16),
        "final_norm": row(final_norm),
    }

    conv_zero = jnp.zeros((depth, bp, CONV_W - 1, state_conv.shape[-1]), F32)
    tm_prompt = min(tp, 256)
    y_p, k_p, v_p, conv_p, _ = _forward(
        x_prompt, mods[:, :bp], conv_zero, None, None, wts,
        tm=tm_prompt, chunk=min(tp, GMLP_CHUNK), emit_v=False)
    y_s, k_s, v_s, conv_s, gv_s = _forward(
        x_sample, mods[:, bp:bp + bs], state_conv, cache_k, cache_v, wts,
        tm=ts, chunk=min(ts, GMLP_CHUNK), emit_v=True)
    return (y_p, y_s, k_p, v_p, k_s, v_s, conv_p, conv_s, gv_s[None])
```

```python
import functools

import jax
import jax.numpy as jnp
from jax import lax
from jax.experimental import pallas as pl
from jax.experimental.pallas import tpu as pltpu

F32 = jnp.float32
BF16 = jnp.bfloat16

EPS = 1e-6
GMLP_CHUNK = 128
GMLP_GROUPS = 8
SB_HEADS = 16
SB_HEAD_DIM = 64
CONV_W = 3
N_ADA = 6

LANES = 128
ADA_ROWS = 16
ADA_TILE_N = 1536
FFN_ROWS = 256
GMLP_ROWS = 512
ATTN_BLOCK = 128
ATTN_WINDOW = 3 * ATTN_BLOCK
ATTN_SUFFIX = 2 * ATTN_BLOCK
ATTN_LANES = 2 * LANES
LOG2E = 1.4426950408889634
ATTN_DEAD_NEG_LOG2 = 110.0 * LOG2E
ATTN_MASKED_LOGIT = -1e30
VMEM_LIMIT_BYTES = 56 * 1024 * 1024


def _params(semantics):
    return pltpu.CompilerParams(dimension_semantics=semantics, vmem_limit_bytes=VMEM_LIMIT_BYTES)


def _resident(shape):
    zeros = (0,) * len(shape)
    return pl.BlockSpec(shape, lambda *_: zeros, pipeline_mode=pl.Buffered(1))


def _rms(x, g):
    return x * lax.rsqrt(jnp.mean(x * x, axis=-1, keepdims=True) + EPS) * g


def _dot(a, b):
    return jnp.dot(a, b, preferred_element_type=F32)


def _ada_kernel(c_ref, w_ref, b_ref, o_ref):
    c = c_ref[...]
    s = (c * jax.nn.sigmoid(c)).astype(BF16)
    o_ref[0] = _dot(s, w_ref[0].astype(BF16)) + b_ref[0]


def _ada_call(c_rows, w, b):
    n_layers, d, n = w.shape
    tn = ADA_TILE_N if n % ADA_TILE_N == 0 else n
    return pl.pallas_call(
        _ada_kernel,
        grid=(n_layers, n // tn),
        in_specs=[
            pl.BlockSpec((ADA_ROWS, d), lambda l, j: (0, 0)),
            pl.BlockSpec((1, d, tn), lambda l, j: (l, 0, j)),
            pl.BlockSpec((1, 1, tn), lambda l, j: (l, 0, j)),
        ],
        out_specs=pl.BlockSpec((1, ADA_ROWS, tn), lambda l, j: (l, 0, j)),
        out_shape=jax.ShapeDtypeStruct((n_layers, ADA_ROWS, n), F32),
        compiler_params=_params(("parallel", "parallel")),
        name="ada_proj",
    )(c_rows, w, b.reshape(n_layers, 1, n))


def _gmlp_kernel(x_ref, mod_ref, gn_ref, win_ref, gv_ref, ws_ref, bs_ref, wout_ref, *rest,
                 chunk, emit_v):
    if emit_v:
        xo_ref, vo_ref, z_ref = rest
    else:
        xo_ref, z_ref = rest
    x = x_ref[0]
    mod = mod_ref[0]
    shift, scale, gate = mod[0:1], mod[1:2], mod[2:3]
    h = (_rms(x, gn_ref[...]) * (1.0 + scale) + shift).astype(BF16)
    uv = jax.nn.gelu(_dot(h, win_ref[...]))
    half = uv.shape[1] // 2
    u = uv[:, :half]
    v = _rms(uv[:, half:], gv_ref[...])
    if emit_v:
        vo_ref[0] = v
    vb = v.astype(BF16)
    gdim = half // GMLP_GROUPS
    row = lax.broadcasted_iota(jnp.int32, (chunk, chunk), 0)
    col = lax.broadcasted_iota(jnp.int32, (chunk, chunk), 1)
    causal = row >= col
    for g in range(GMLP_GROUPS):
        w = jnp.where(causal, ws_ref[g], 0.0).astype(BF16)
        bias = bs_ref[:, g:g + 1]
        cs = slice(g * gdim, (g + 1) * gdim)
        for c in range(x.shape[0] // chunk):
            rs = slice(c * chunk, (c + 1) * chunk)
            mixed = _dot(w, vb[rs, cs]) + bias
            z_ref[rs, cs] = (u[rs, cs] * mixed).astype(BF16)
    y = _dot(z_ref[...], wout_ref[...])
    xo_ref[0] = x + gate * y


def _gmlp_call(x, mod, gn, w_in, g_v, w_s, b_s_t, w_out, *, tm, chunk, emit_v):
    b, t, d = x.shape
    u2 = w_in.shape[1]
    half = u2 // 2
    x_spec = pl.BlockSpec((1, tm, d), lambda i, j: (i, j, 0))
    out_specs = [x_spec]
    out_shape = [jax.ShapeDtypeStruct((b, t, d), F32)]
    if emit_v:
        out_specs.append(pl.BlockSpec((1, tm, half), lambda i, j: (i, j, 0)))
        out_shape.append(jax.ShapeDtypeStruct((b, t, half), F32))
    return pl.pallas_call(
        functools.partial(_gmlp_kernel, chunk=chunk, emit_v=emit_v),
        grid=(b, t // tm),
        in_specs=[
            x_spec,
            pl.BlockSpec((1, 8, d), lambda i, j: (i, 0, 0)),
            _resident((1, d)),
            _resident((d, u2)),
            _resident((1, half)),
            _resident((GMLP_GROUPS, chunk, chunk)),
            _resident((chunk, GMLP_GROUPS)),
            _resident((half, d)),
        ],
        out_specs=out_specs,
        out_shape=out_shape,
        scratch_shapes=[pltpu.VMEM((tm, half), BF16)],
        compiler_params=_params(("parallel", "parallel")),
        name="gmlp_layer",
    )(x, mod, gn, w_in, g_v, w_s, b_s_t, w_out)


def _conv_ffn(x, shift, scale, gate, gn, w_up, conv_w, conv_b, w_down, hist_ref, row_ref):
    tm = x.shape[0]
    h = (_rms(x, gn) * (1.0 + scale) + shift).astype(BF16)
    ag = _dot(h, w_up)
    f = ag.shape[1] // 2
    a = ag[:, :f]
    g = ag[:, f:]
    prev1 = hist_ref[1:2, :]
    prev2 = hist_ref[0:1, :]
    row = lax.broadcasted_iota(jnp.int32, (tm, 1), 0)
    g1 = jnp.where(row == 0, prev1, pltpu.roll(g, 1, axis=0))
    g2 = jnp.where(row == 0, prev2, jnp.where(row == 1, prev1, pltpu.roll(g, 2, axis=0)))
    gc = conv_b + conv_w[0:1] * g2 + conv_w[1:2] * g1 + conv_w[2:3] * g
    last = g[tm - (CONV_W - 1):, :]
    hist_ref[...] = last
    row_ref[0] = last
    y = _dot((gc * jax.nn.sigmoid(gc) * a).astype(BF16), w_down)
    return x + gate * y


def _ffn0_kernel(x_ref, mod_ref, hist_in_ref, gn_ref, wup_ref, cw_ref, cb_ref, wdown_ref,
                 gkv_ref, wkv_ref, gq_ref, wq_ref,
                 xo_ref, row_ref, k_ref, v_ref, kb_ref, vb_ref, qb_ref, hist_ref):
    @pl.when(pl.program_id(1) == 0)
    def _():
        hist_ref[...] = hist_in_ref[0]

    mod = mod_ref[0, 0]
    mod_next = mod_ref[1, 0]
    x = _conv_ffn(x_ref[0], mod[3:4], mod[4:5], mod[5:6], gn_ref[...], wup_ref[...], cw_ref[...],
                  cb_ref[...], wdown_ref[...], hist_ref, row_ref)
    xo_ref[0] = x
    hk = (_rms(x, gkv_ref[...]) * (1.0 + mod[7:8]) + mod[6:7]).astype(BF16)
    kv = _dot(hk, wkv_ref[...])
    hd = kv.shape[1] // 2
    k_ref[0] = kv[:, :hd]
    v_ref[0] = kv[:, hd:]
    kb_ref[0] = kv[:, :hd].astype(BF16)
    vb_ref[0] = kv[:, hd:].astype(BF16)
    hq = (_rms(x, gq_ref[...]) * (1.0 + mod_next[1:2]) + mod_next[0:1]).astype(BF16)
    qb_ref[0] = (_dot(hq, wq_ref[...]) * (SB_HEAD_DIM ** -0.5 * LOG2E)).astype(BF16)


def _ffn1_kernel(x_ref, o_ref, mod_ref, hist_in_ref, wo_ref, gn_ref, wup_ref, cw_ref, cb_ref,
                 wdown_ref, gfin_ref, y_ref, row_ref, hist_ref):
    @pl.when(pl.program_id(1) == 0)
    def _():
        hist_ref[...] = hist_in_ref[0]

    mod = mod_ref[0]
    x = x_ref[0] + mod[2:3] * _dot(o_ref[0], wo_ref[...])
    x = _conv_ffn(x, mod[3:4], mod[4:5], mod[5:6], gn_ref[...], wup_ref[...], cw_ref[...],
                  cb_ref[...], wdown_ref[...], hist_ref, row_ref)
    y_ref[0] = _rms(x, gfin_ref[...])


def _ffn0_call(x, mod01, hist, gn, w_up, conv_w, conv_b, w_down, g_kv, w_kv, g_q, w_q, *, tm):
    b, t, d = x.shape
    f = w_down.shape[0]
    hd = w_q.shape[1]
    row_block = lambda last, dt: (pl.BlockSpec((1, tm, last), lambda i, j: (i, j, 0)),
                                  jax.ShapeDtypeStruct((b, t, last), dt))
    outs = [row_block(d, F32),
            (pl.BlockSpec((1, CONV_W - 1, f), lambda i, j: (i, 0, 0)),
             jax.ShapeDtypeStruct((b, CONV_W - 1, f), F32)),
            row_block(hd, F32), row_block(hd, F32),
            row_block(hd, BF16), row_block(hd, BF16), row_block(hd, BF16)]
    return pl.pallas_call(
        _ffn0_kernel,
        grid=(b, t // tm),
        in_specs=[
            pl.BlockSpec((1, tm, d), lambda i, j: (i, j, 0)),
            pl.BlockSpec((2, 1, 8, d), lambda i, j: (0, i, 0, 0)),
            pl.BlockSpec((1, CONV_W - 1, f), lambda i, j: (i, 0, 0)),
            _resident((1, d)), _resident((d, 2 * f)), _resident((CONV_W, f)), _resident((1, f)),
            _resident((f, d)), _resident((1, d)), _resident((d, 2 * hd)), _resident((1, d)),
            _resident((d, hd)),
        ],
        out_specs=[o[0] for o in outs],
        out_shape=[o[1] for o in outs],
        scratch_shapes=[pltpu.VMEM((CONV_W - 1, f), F32)],
        compiler_params=_params(("parallel", "arbitrary")),
        name="ffn0_kv_q",
    )(x, mod01, hist, gn, w_up, conv_w, conv_b, w_down, g_kv, w_kv, g_q, w_q)


def _ffn1_call(x, o, mod, hist, w_o, gn, w_up, conv_w, conv_b, w_down, g_fin, *, tm):
    b, t, d = x.shape
    f = w_down.shape[0]
    hd = w_o.shape[0]
    return pl.pallas_call(
        _ffn1_kernel,
        grid=(b, t // tm),
        in_specs=[
            pl.BlockSpec((1, tm, d), lambda i, j: (i, j, 0)),
            pl.BlockSpec((1, tm, hd), lambda i, j: (i, j, 0)),
            pl.BlockSpec((1, 8, d), lambda i, j: (i, 0, 0)),
            pl.BlockSpec((1, CONV_W - 1, f), lambda i, j: (i, 0, 0)),
            _resident((hd, d)), _resident((1, d)), _resident((d, 2 * f)), _resident((CONV_W, f)),
            _resident((1, f)), _resident((f, d)), _resident((1, d)),
        ],
        out_specs=[pl.BlockSpec((1, tm, d), lambda i, j: (i, j, 0)),
                   pl.BlockSpec((1, CONV_W - 1, f), lambda i, j: (i, 0, 0))],
        out_shape=[jax.ShapeDtypeStruct((b, t, d), F32),
                   jax.ShapeDtypeStruct((b, CONV_W - 1, f), F32)],
        scratch_shapes=[pltpu.VMEM((CONV_W - 1, f), F32)],
        compiler_params=_params(("parallel", "arbitrary")),
        name="ffn1_final",
    )(x, o, mod, hist, w_o, gn, w_up, conv_w, conv_b, w_down, g_fin)


def _attn_kernel(q_ref, k_ref, v_ref, sfx_ref, o_ref, *, tq, q_pos0):
    tk, win, sfx_n = ATTN_BLOCK, ATTN_WINDOW, ATTN_SUFFIX
    groups = ATTN_LANES // LANES
    q_start = q_pos0 + pl.program_id(2) * tq
    win_end = (q_start // tk + 1) * tk
    win_start = jnp.maximum(win_end - win, 0)
    lane = lax.broadcasted_iota(jnp.int32, (1, LANES), 1)
    head0 = lane < SB_HEAD_DIM
    t_iota = lax.broadcasted_iota(jnp.int32, (tq, 1), 0)
    t_pos = q_start + jnp.concatenate([t_iota, t_iota], axis=0)

    def stacked_q(g):
        q = q_ref[0, :, g * LANES:(g + 1) * LANES]
        zero = jnp.zeros_like(q)
        return jnp.concatenate([jnp.where(head0, q, zero), jnp.where(head0, zero, q)], axis=0)

    q2s = [stacked_q(g) for g in range(groups)]

    def suffix_sums(nl):
        width = nl.shape[1]
        hi = nl.astype(BF16)
        lo = (nl - hi.astype(F32)).astype(BF16)
        parts = jnp.concatenate([hi, lo], axis=0)
        if width > sfx_n:
            cut = width - sfx_n
            new = _dot(parts[:, cut:], sfx_ref[...])
            old = _dot(parts[:, :cut], sfx_ref[0:cut, 0:cut]) + new[:, 0:1]
            sums = jnp.concatenate([old, new], axis=1)
        else:
            sums = _dot(parts, sfx_ref[0:width, 0:width])
        return sums[:2 * tq] + sums[2 * tq:]

    def block(g, start, width, o, nacc, masked):
        k_blk = k_ref[0, pl.ds(start, width), g * LANES:(g + 1) * LANES]
        v_blk = v_ref[0, pl.ds(start, width), g * LANES:(g + 1) * LANES]
        z = lax.dot_general(q2s[g], k_blk, (((1,), (1,)), ((), ())), preferred_element_type=F32)
        if masked:
            mask = (start + lax.broadcasted_iota(jnp.int32, (1, width), 1)) < t_pos
            z = jnp.where(mask, z, ATTN_MASKED_LOGIT)
        nl = jnp.maximum(z, 0.0) + jnp.log(1.0 + jnp.exp2(-jnp.abs(z))) * LOG2E
        incl = suffix_sums(nl)
        a = jnp.exp2(z - incl - nacc)
        pv = _dot(a.astype(BF16), v_blk)
        o = o + jnp.where(head0, pv[:tq], pv[tq:])
        return o, nacc + incl[:, 0:1]

    def alive(naccs):
        low = functools.reduce(jnp.minimum, naccs)
        return (jnp.min(low) < ATTN_DEAD_NEG_LOG2).astype(jnp.int32)

    first = [block(g, pl.multiple_of(win_start, tk), win, jnp.zeros((tq, LANES), F32),
                   jnp.zeros((2 * tq, 1), F32), True) for g in range(groups)]
    os, naccs = [f[0] for f in first], [f[1] for f in first]

    def cond(carry):
        return jnp.logical_and(carry[0] >= 0, carry[1] > 0)

    def body(carry):
        kb, _, os, naccs = carry
        nxt = [block(g, pl.multiple_of(kb * tk, tk), tk, os[g], naccs[g], False) for g in range(groups)]
        os, naccs = [n[0] for n in nxt], [n[1] for n in nxt]
        return kb - 1, alive(naccs), os, naccs

    carry = lax.while_loop(cond, body, (win_start // tk - 1, alive(naccs), os, naccs))
    o_ref[0] = jnp.concatenate(carry[2], axis=1).astype(o_ref.dtype)


def _attn_call(q, k, v, *, tq, q_pos0):
    b, t, hd = q.shape
    s = k.shape[1]
    assert ATTN_BLOCK % tq == 0 and q_pos0 % tq == 0 and s % ATTN_BLOCK == 0 and s >= ATTN_WINDOW
    assert hd % ATTN_LANES == 0 and ATTN_WINDOW - ATTN_SUFFIX <= ATTN_SUFFIX
    idx = jnp.arange(ATTN_SUFFIX)
    suffix = (idx[:, None] >= idx[None, :]).astype(BF16)
    kv_spec = pl.BlockSpec((1, s, ATTN_LANES), lambda i, h, j: (i, 0, h))
    q_spec = pl.BlockSpec((1, tq, ATTN_LANES), lambda i, h, j: (i, j, h))
    return pl.pallas_call(
        functools.partial(_attn_kernel, tq=tq, q_pos0=q_pos0),
        grid=(b, hd // ATTN_LANES, t // tq),
        in_specs=[q_spec, kv_spec, kv_spec, _resident((ATTN_SUFFIX, ATTN_SUFFIX))],
        out_specs=q_spec,
        out_shape=jax.ShapeDtypeStruct((b, t, hd), BF16),
        compiler_params=_params(("parallel", "parallel", "parallel")),
        name="sb_attention",
    )(q, k, v, suffix)


def _forward(x, mods, conv_hist, cache_k, cache_v, wts, *, tm, tm_gmlp, chunk, emit_v):
    b, t, d = x.shape
    x1 = _gmlp_call(x, mods[0], wts["norm_mix0"], wts["gmlp_w_in"], wts["gmlp_norm_v"],
                    wts["gmlp_w_s"][:, :chunk, :chunk], wts["gmlp_b_s"][:, :chunk].T,
                    wts["gmlp_w_out"], tm=tm_gmlp, chunk=chunk, emit_v=emit_v)
    v_rows = None
    if emit_v:
        x1, v_rows = x1
    else:
        (x1,) = x1
    x2, conv0, k, v, kb, vb, qb = _ffn0_call(
        x1, mods, conv_hist[0], wts["norm_ffn0"], wts["ffn_w_up0"], wts["ffn_conv_w"][0],
        wts["ffn_conv_b0"], wts["ffn_w_down0"], wts["kv_norm"], wts["w_kv"], wts["norm_mix1"],
        wts["sb_w_q"], tm=tm)
    if cache_k is None:
        o = _attn_call(qb, kb, vb, tq=ATTN_BLOCK, q_pos0=0)
    else:
        past = cache_k.shape[1]
        pad = -(past + t) % ATTN_BLOCK
        def all_keys(cache, new):
            rows = jnp.concatenate([cache.reshape(b, past, -1).astype(BF16), new], axis=1)
            return jnp.pad(rows, ((0, 0), (0, pad), (0, 0)))
        o = _attn_call(qb, all_keys(cache_k, kb), all_keys(cache_v, vb), tq=t, q_pos0=past)
    y, conv1 = _ffn1_call(x2, o, mods[1], conv_hist[1], wts["sb_w_o"], wts["norm_ffn1"],
                          wts["ffn_w_up1"], wts["ffn_conv_w"][1], wts["ffn_conv_b1"],
                          wts["ffn_w_down1"], wts["final_norm"], tm=tm)
    heads = (b, t, SB_HEADS, SB_HEAD_DIM)
    return y, k.reshape(heads), v.reshape(heads), jnp.stack([conv0, conv1]), v_rows


def kernel(x_prompt, x_sample, c_prompt, c_sample, cache_k, cache_v, state_conv, ada_w, ada_b, norm_mix, norm_ffn, gmlp_w_in, gmlp_norm_v, gmlp_w_s, gmlp_b_s, gmlp_w_out, kv_ada_w, kv_ada_b, kv_norm, w_kv, sb_w_q, sb_w_o, ffn_w_up, ffn_conv_w, ffn_conv_b, ffn_w_down, final_norm):
    depth, d = norm_mix.shape
    assert depth == 2 and gmlp_w_in.shape[0] == 1 and sb_w_q.shape[0] == 1
    bp, tp, _ = x_prompt.shape
    bs, ts, _ = x_sample.shape
    assert bp + bs <= ADA_ROWS

    c_rows = jnp.zeros((ADA_ROWS, d), F32).at[:bp].set(c_prompt).at[bp:bp + bs].set(c_sample)
    ada = _ada_call(c_rows, ada_w, ada_b)
    kv_ada = _ada_call(c_rows, kv_ada_w[None], kv_ada_b[None])
    ada = ada.reshape(depth, ADA_ROWS, N_ADA, d)
    kv_rows = jnp.broadcast_to(kv_ada.reshape(1, ADA_ROWS, 2, d), (depth, ADA_ROWS, 2, d))
    mods = jnp.concatenate([ada, kv_rows], axis=2)

    row = lambda a: a.reshape(1, -1)
    wts = {
        "norm_mix0": row(norm_mix[0]), "norm_mix1": row(norm_mix[1]),
        "norm_ffn0": row(norm_ffn[0]), "norm_ffn1": row(norm_ffn[1]),
        "gmlp_w_in": gmlp_w_in[0].astype(BF16), "gmlp_norm_v": row(gmlp_norm_v[0]),
        "gmlp_w_s": gmlp_w_s[0], "gmlp_b_s": gmlp_b_s[0], "gmlp_w_out": gmlp_w_out[0].astype(BF16),
        "kv_norm": row(kv_norm), "w_kv": w_kv.astype(BF16),
        "sb_w_q": sb_w_q[0].astype(BF16), "sb_w_o": sb_w_o[0].astype(BF16),
        "ffn_w_up0": ffn_w_up[0].astype(BF16), "ffn_w_up1": ffn_w_up[1].astype(BF16),
        "ffn_conv_w": ffn_conv_w, "ffn_conv_b0": row(ffn_conv_b[0]), "ffn_conv_b1": row(ffn_conv_b[1]),
        "ffn_w_down0": ffn_w_down[0].astype(BF16), "ffn_w_down1": ffn_w_down[1].astype(BF16),
        "final_norm": row(final_norm),
    }

    conv_zero = jnp.zeros((depth, bp, CONV_W - 1, state_conv.shape[-1]), F32)
    y_p, k_p, v_p, conv_p, _ = _forward(
        x_prompt, mods[:, :bp], conv_zero, None, None, wts,
        tm=min(tp, FFN_ROWS), tm_gmlp=min(tp, GMLP_ROWS), chunk=min(tp, GMLP_CHUNK), emit_v=False)
    y_s, k_s, v_s, conv_s, gv_s = _forward(
        x_sample, mods[:, bp:bp + bs], state_conv, cache_k, cache_v, wts,
        tm=ts, tm_gmlp=ts, chunk=min(ts, GMLP_CHUNK), emit_v=True)
    return (y_p, y_s, k_p, v_p, k_s, v_s, conv_p, conv_s, gv_s[None])
```

```python
import functools

import jax
import jax.numpy as jnp
from jax import lax
from jax.experimental import pallas as pl
from jax.experimental.pallas import tpu as pltpu

F32 = jnp.float32
BF16 = jnp.bfloat16

EPS = 1e-6
GMLP_CHUNK = 128
GMLP_GROUPS = 8
SB_HEADS = 16
SB_HEAD_DIM = 64
CONV_W = 3
N_ADA = 6

LANES = 128
ADA_ROWS = 16
ADA_TILE_N = 1536
FFN_ROWS = 256
FFN1_ROWS = 512
GMLP_ROWS = 512
ATTN_BLOCK = 128
ATTN_WINDOW = 3 * ATTN_BLOCK
ATTN_SUFFIX = 2 * ATTN_BLOCK
ATTN_LANES = 2 * LANES
LOG2E = 1.4426950408889634
ATTN_DEAD_NEG_LOG2 = 110.0 * LOG2E
ATTN_MASKED_LOGIT = -1e30
VMEM_LIMIT_BYTES = 56 * 1024 * 1024


def _params(semantics):
    return pltpu.CompilerParams(dimension_semantics=semantics, vmem_limit_bytes=VMEM_LIMIT_BYTES)


def _resident(shape):
    zeros = (0,) * len(shape)
    return pl.BlockSpec(shape, lambda *_: zeros, pipeline_mode=pl.Buffered(1))


def _rms(x, g):
    return x * lax.rsqrt(jnp.mean(x * x, axis=-1, keepdims=True) + EPS) * g


def _dot(a, b):
    return jnp.dot(a, b, preferred_element_type=F32)


def _ada_kernel(c_ref, w_ref, b_ref, o_ref):
    c = c_ref[...]
    s = (c * jax.nn.sigmoid(c)).astype(BF16)
    o_ref[0] = _dot(s, w_ref[0].astype(BF16)) + b_ref[0]


def _ada_call(c_rows, w, b):
    n_layers, d, n = w.shape
    tn = ADA_TILE_N if n % ADA_TILE_N == 0 else n
    return pl.pallas_call(
        _ada_kernel,
        grid=(n_layers, n // tn),
        in_specs=[
            pl.BlockSpec((ADA_ROWS, d), lambda l, j: (0, 0)),
            pl.BlockSpec((1, d, tn), lambda l, j: (l, 0, j)),
            pl.BlockSpec((1, 1, tn), lambda l, j: (l, 0, j)),
        ],
        out_specs=pl.BlockSpec((1, ADA_ROWS, tn), lambda l, j: (l, 0, j)),
        out_shape=jax.ShapeDtypeStruct((n_layers, ADA_ROWS, n), F32),
        compiler_params=_params(("parallel", "parallel")),
        name="ada_proj",
    )(c_rows, w, b.reshape(n_layers, 1, n))


def _gmlp_kernel(x_ref, mod_ref, gn_ref, win_ref, gv_ref, ws_ref, bs_ref, wout_ref, *rest,
                 chunk, emit_v):
    if emit_v:
        xo_ref, vo_ref, z_ref = rest
    else:
        xo_ref, z_ref = rest
    x = x_ref[0]
    mod = mod_ref[0]
    shift, scale, gate = mod[0:1], mod[1:2], mod[2:3]
    h = (_rms(x, gn_ref[...]) * (1.0 + scale) + shift).astype(BF16)
    uv = jax.nn.gelu(_dot(h, win_ref[...]))
    half = uv.shape[1] // 2
    u = uv[:, :half]
    v = _rms(uv[:, half:], gv_ref[...])
    if emit_v:
        vo_ref[0] = v
    vb = v.astype(BF16)
    gdim = half // GMLP_GROUPS
    row = lax.broadcasted_iota(jnp.int32, (chunk, chunk), 0)
    col = lax.broadcasted_iota(jnp.int32, (chunk, chunk), 1)
    causal = row >= col
    for g in range(GMLP_GROUPS):
        w = jnp.where(causal, ws_ref[g], 0.0).astype(BF16)
        bias = bs_ref[:, g:g + 1]
        cs = slice(g * gdim, (g + 1) * gdim)
        for c in range(x.shape[0] // chunk):
            rs = slice(c * chunk, (c + 1) * chunk)
            mixed = _dot(w, vb[rs, cs]) + bias
            z_ref[rs, cs] = (u[rs, cs] * mixed).astype(BF16)
    y = _dot(z_ref[...], wout_ref[...])
    xo_ref[0] = x + gate * y


def _gmlp_call(x, mod, gn, w_in, g_v, w_s, b_s_t, w_out, *, tm, chunk, emit_v):
    b, t, d = x.shape
    u2 = w_in.shape[1]
    half = u2 // 2
    x_spec = pl.BlockSpec((1, tm, d), lambda i, j: (i, j, 0))
    out_specs = [x_spec]
    out_shape = [jax.ShapeDtypeStruct((b, t, d), F32)]
    if emit_v:
        out_specs.append(pl.BlockSpec((1, tm, half), lambda i, j: (i, j, 0)))
        out_shape.append(jax.ShapeDtypeStruct((b, t, half), F32))
    return pl.pallas_call(
        functools.partial(_gmlp_kernel, chunk=chunk, emit_v=emit_v),
        grid=(b, t // tm),
        in_specs=[
            x_spec,
            pl.BlockSpec((1, 8, d), lambda i, j: (i, 0, 0)),
            _resident((1, d)),
            _resident((d, u2)),
            _resident((1, half)),
            _resident((GMLP_GROUPS, chunk, chunk)),
            _resident((chunk, GMLP_GROUPS)),
            _resident((half, d)),
        ],
        out_specs=out_specs,
        out_shape=out_shape,
        scratch_shapes=[pltpu.VMEM((tm, half), BF16)],
        compiler_params=_params(("parallel", "parallel")),
        name="gmlp_layer",
    )(x, mod, gn, w_in, g_v, w_s, b_s_t, w_out)


def _conv_ffn(x, shift, scale, gate, gn, w_up, conv_w, conv_b, w_down, hist_ref, row_ref):
    tm = x.shape[0]
    h = (_rms(x, gn) * (1.0 + scale) + shift).astype(BF16)
    ag = _dot(h, w_up)
    f = ag.shape[1] // 2
    a = ag[:, :f]
    g = ag[:, f:]
    prev1 = hist_ref[1:2, :]
    prev2 = hist_ref[0:1, :]
    row = lax.broadcasted_iota(jnp.int32, (tm, 1), 0)
    g1 = jnp.where(row == 0, prev1, pltpu.roll(g, 1, axis=0))
    g2 = jnp.where(row == 0, prev2, jnp.where(row == 1, prev1, pltpu.roll(g, 2, axis=0)))
    gc = conv_b + conv_w[0:1] * g2 + conv_w[1:2] * g1 + conv_w[2:3] * g
    last = g[tm - (CONV_W - 1):, :]
    hist_ref[...] = last
    row_ref[0] = last
    y = _dot((gc * jax.nn.sigmoid(gc) * a).astype(BF16), w_down)
    return x + gate * y


def _ffn0_kernel(x_ref, mod_ref, hist_in_ref, gn_ref, wup_ref, cw_ref, cb_ref, wdown_ref,
                 gkv_ref, wkv_ref, gq_ref, wq_ref,
                 xo_ref, row_ref, k_ref, v_ref, kb_ref, vb_ref, qb_ref, hist_ref):
    @pl.when(pl.program_id(1) == 0)
    def _():
        hist_ref[...] = hist_in_ref[0]

    mod = mod_ref[0, 0]
    mod_next = mod_ref[1, 0]
    x = _conv_ffn(x_ref[0], mod[3:4], mod[4:5], mod[5:6], gn_ref[...], wup_ref[...], cw_ref[...],
                  cb_ref[...], wdown_ref[...], hist_ref, row_ref)
    xo_ref[0] = x
    hk = (_rms(x, gkv_ref[...]) * (1.0 + mod[7:8]) + mod[6:7]).astype(BF16)
    kv = _dot(hk, wkv_ref[...])
    hd = kv.shape[1] // 2
    k_ref[0] = kv[:, :hd]
    v_ref[0] = kv[:, hd:]
    kb_ref[0] = kv[:, :hd].astype(BF16)
    vb_ref[0] = kv[:, hd:].astype(BF16)
    hq = (_rms(x, gq_ref[...]) * (1.0 + mod_next[1:2]) + mod_next[0:1]).astype(BF16)
    qb_ref[0] = (_dot(hq, wq_ref[...]) * (SB_HEAD_DIM ** -0.5 * LOG2E)).astype(BF16)


def _ffn1_kernel(x_ref, o_ref, mod_ref, hist_in_ref, wo_ref, gn_ref, wup_ref, cw_ref, cb_ref,
                 wdown_ref, gfin_ref, y_ref, row_ref, hist_ref):
    @pl.when(pl.program_id(1) == 0)
    def _():
        hist_ref[...] = hist_in_ref[0]

    mod = mod_ref[0]
    x = x_ref[0] + mod[2:3] * _dot(o_ref[0], wo_ref[...])
    x = _conv_ffn(x, mod[3:4], mod[4:5], mod[5:6], gn_ref[...], wup_ref[...], cw_ref[...],
                  cb_ref[...], wdown_ref[...], hist_ref, row_ref)
    y_ref[0] = _rms(x, gfin_ref[...])


def _ffn0_call(x, mod01, hist, gn, w_up, conv_w, conv_b, w_down, g_kv, w_kv, g_q, w_q, *, tm):
    b, t, d = x.shape
    f = w_down.shape[0]
    hd = w_q.shape[1]
    row_block = lambda last, dt: (pl.BlockSpec((1, tm, last), lambda i, j: (i, j, 0)),
                                  jax.ShapeDtypeStruct((b, t, last), dt))
    outs = [row_block(d, F32),
            (pl.BlockSpec((1, CONV_W - 1, f), lambda i, j: (i, 0, 0)),
             jax.ShapeDtypeStruct((b, CONV_W - 1, f), F32)),
            row_block(hd, F32), row_block(hd, F32),
            row_block(hd, BF16), row_block(hd, BF16), row_block(hd, BF16)]
    return pl.pallas_call(
        _ffn0_kernel,
        grid=(b, t // tm),
        in_specs=[
            pl.BlockSpec((1, tm, d), lambda i, j: (i, j, 0)),
            pl.BlockSpec((2, 1, 8, d), lambda i, j: (0, i, 0, 0)),
            pl.BlockSpec((1, CONV_W - 1, f), lambda i, j: (i, 0, 0)),
            _resident((1, d)), _resident((d, 2 * f)), _resident((CONV_W, f)), _resident((1, f)),
            _resident((f, d)), _resident((1, d)), _resident((d, 2 * hd)), _resident((1, d)),
            _resident((d, hd)),
        ],
        out_specs=[o[0] for o in outs],
        out_shape=[o[1] for o in outs],
        scratch_shapes=[pltpu.VMEM((CONV_W - 1, f), F32)],
        compiler_params=_params(("parallel", "arbitrary")),
        name="ffn0_kv_q",
    )(x, mod01, hist, gn, w_up, conv_w, conv_b, w_down, g_kv, w_kv, g_q, w_q)


def _ffn1_call(x, o, mod, hist, w_o, gn, w_up, conv_w, conv_b, w_down, g_fin, *, tm):
    b, t, d = x.shape
    f = w_down.shape[0]
    hd = w_o.shape[0]
    return pl.pallas_call(
        _ffn1_kernel,
        grid=(b, t // tm),
        in_specs=[
            pl.BlockSpec((1, tm, d), lambda i, j: (i, j, 0)),
            pl.BlockSpec((1, tm, hd), lambda i, j: (i, j, 0)),
            pl.BlockSpec((1, 8, d), lambda i, j: (i, 0, 0)),
            pl.BlockSpec((1, CONV_W - 1, f), lambda i, j: (i, 0, 0)),
            _resident((hd, d)), _resident((1, d)), _resident((d, 2 * f)), _resident((CONV_W, f)),
            _resident((1, f)), _resident((f, d)), _resident((1, d)),
        ],
        out_specs=[pl.BlockSpec((1, tm, d), lambda i, j: (i, j, 0)),
                   pl.BlockSpec((1, CONV_W - 1, f), lambda i, j: (i, 0, 0))],
        out_shape=[jax.ShapeDtypeStruct((b, t, d), F32),
                   jax.ShapeDtypeStruct((b, CONV_W - 1, f), F32)],
        scratch_shapes=[pltpu.VMEM((CONV_W - 1, f), F32)],
        compiler_params=_params(("parallel", "arbitrary")),
        name="ffn1_final",
    )(x, o, mod, hist, w_o, gn, w_up, conv_w, conv_b, w_down, g_fin)


def _attn_kernel(q_ref, k_ref, v_ref, sfx_ref, o_ref, *, tq, q_pos0):
    tk, win, sfx_n = ATTN_BLOCK, ATTN_WINDOW, ATTN_SUFFIX
    groups = ATTN_LANES // LANES
    q_start = q_pos0 + pl.program_id(2) * tq
    win_end = (q_start // tk + 1) * tk
    win_start = jnp.maximum(win_end - win, 0)
    lane = lax.broadcasted_iota(jnp.int32, (1, LANES), 1)
    head0 = lane < SB_HEAD_DIM
    t_iota = lax.broadcasted_iota(jnp.int32, (tq, 1), 0)
    t_pos = q_start + jnp.concatenate([t_iota, t_iota], axis=0)

    def stacked_q(g):
        q = q_ref[0, :, g * LANES:(g + 1) * LANES]
        zero = jnp.zeros_like(q)
        return jnp.concatenate([jnp.where(head0, q, zero), jnp.where(head0, zero, q)], axis=0)

    q2s = [stacked_q(g) for g in range(groups)]

    def suffix_sums(nl):
        width = nl.shape[1]
        hi = nl.astype(BF16)
        lo = (nl - hi.astype(F32)).astype(BF16)
        parts = jnp.concatenate([hi, lo], axis=0)
        if width > sfx_n:
            cut = width - sfx_n
            new = _dot(parts[:, cut:], sfx_ref[...])
            old = _dot(parts[:, :cut], sfx_ref[0:cut, 0:cut]) + new[:, 0:1]
            sums = jnp.concatenate([old, new], axis=1)
        else:
            sums = _dot(parts, sfx_ref[0:width, 0:width])
        return sums[:2 * tq] + sums[2 * tq:]

    def block(g, start, width, o, nacc, masked):
        k_blk = k_ref[0, pl.ds(start, width), g * LANES:(g + 1) * LANES]
        v_blk = v_ref[0, pl.ds(start, width), g * LANES:(g + 1) * LANES]
        z = lax.dot_general(q2s[g], k_blk, (((1,), (1,)), ((), ())), preferred_element_type=F32)
        if masked:
            mask = (start + lax.broadcasted_iota(jnp.int32, (1, width), 1)) < t_pos
            z = jnp.where(mask, z, ATTN_MASKED_LOGIT)
        nl = jnp.maximum(z, 0.0) + jnp.log(1.0 + jnp.exp2(-jnp.abs(z))) * LOG2E
        incl = suffix_sums(nl)
        a = jnp.exp2(z - incl - nacc)
        pv = _dot(a.astype(BF16), v_blk)
        o = o + jnp.where(head0, pv[:tq], pv[tq:])
        return o, nacc + incl[:, 0:1]

    def alive(naccs):
        low = functools.reduce(jnp.minimum, naccs)
        return (jnp.min(low) < ATTN_DEAD_NEG_LOG2).astype(jnp.int32)

    first = [block(g, pl.multiple_of(win_start, tk), win, jnp.zeros((tq, LANES), F32),
                   jnp.zeros((2 * tq, 1), F32), True) for g in range(groups)]
    os, naccs = [f[0] for f in first], [f[1] for f in first]

    def cond(carry):
        return jnp.logical_and(carry[0] >= 0, carry[1] > 0)

    def body(carry):
        kb, _, os, naccs = carry
        nxt = [block(g, pl.multiple_of(kb * tk, tk), tk, os[g], naccs[g], False) for g in range(groups)]
        os, naccs = [n[0] for n in nxt], [n[1] for n in nxt]
        return kb - 1, alive(naccs), os, naccs

    carry = lax.while_loop(cond, body, (win_start // tk - 1, alive(naccs), os, naccs))
    o_ref[0] = jnp.concatenate(carry[2], axis=1).astype(o_ref.dtype)


def _attn_call(q, k, v, *, tq, q_pos0):
    b, t, hd = q.shape
    s = k.shape[1]
    assert ATTN_BLOCK % tq == 0 and q_pos0 % tq == 0 and s % ATTN_BLOCK == 0 and s >= ATTN_WINDOW
    assert hd % ATTN_LANES == 0 and ATTN_WINDOW - ATTN_SUFFIX <= ATTN_SUFFIX
    idx = jnp.arange(ATTN_SUFFIX)
    suffix = (idx[:, None] >= idx[None, :]).astype(BF16)
    kv_spec = pl.BlockSpec((1, s, ATTN_LANES), lambda i, h, j: (i, 0, h))
    q_spec = pl.BlockSpec((1, tq, ATTN_LANES), lambda i, h, j: (i, j, h))
    return pl.pallas_call(
        functools.partial(_attn_kernel, tq=tq, q_pos0=q_pos0),
        grid=(b, hd // ATTN_LANES, t // tq),
        in_specs=[q_spec, kv_spec, kv_spec, _resident((ATTN_SUFFIX, ATTN_SUFFIX))],
        out_specs=q_spec,
        out_shape=jax.ShapeDtypeStruct((b, t, hd), BF16),
        compiler_params=_params(("parallel", "parallel", "parallel")),
        name="sb_attention",
    )(q, k, v, suffix)


def _forward(x, mods, conv_hist, cache_k, cache_v, wts, *, tm, tm_ffn1, tm_gmlp, chunk, emit_v):
    b, t, d = x.shape
    x1 = _gmlp_call(x, mods[0], wts["norm_mix0"], wts["gmlp_w_in"], wts["gmlp_norm_v"],
                    wts["gmlp_w_s"][:, :chunk, :chunk], wts["gmlp_b_s"][:, :chunk].T,
                    wts["gmlp_w_out"], tm=tm_gmlp, chunk=chunk, emit_v=emit_v)
    v_rows = None
    if emit_v:
        x1, v_rows = x1
    else:
        (x1,) = x1
    x2, conv0, k, v, kb, vb, qb = _ffn0_call(
        x1, mods, conv_hist[0], wts["norm_ffn0"], wts["ffn_w_up0"], wts["ffn_conv_w"][0],
        wts["ffn_conv_b0"], wts["ffn_w_down0"], wts["kv_norm"], wts["w_kv"], wts["norm_mix1"],
        wts["sb_w_q"], tm=tm)
    if cache_k is None:
        o = _attn_call(qb, kb, vb, tq=ATTN_BLOCK, q_pos0=0)
    else:
        past = cache_k.shape[1]
        pad = -(past + t) % ATTN_BLOCK
        def all_keys(cache, new):
            rows = jnp.concatenate([cache.reshape(b, past, -1).astype(BF16), new], axis=1)
            return jnp.pad(rows, ((0, 0), (0, pad), (0, 0)))
        o = _attn_call(qb, all_keys(cache_k, kb), all_keys(cache_v, vb), tq=t, q_pos0=past)
    y, conv1 = _ffn1_call(x2, o, mods[1], conv_hist[1], wts["sb_w_o"], wts["norm_ffn1"],
                          wts["ffn_w_up1"], wts["ffn_conv_w"][1], wts["ffn_conv_b1"],
                          wts["ffn_w_down1"], wts["final_norm"], tm=tm_ffn1)
    heads = (b, t, SB_HEADS, SB_HEAD_DIM)
    return y, k.reshape(heads), v.reshape(heads), jnp.stack([conv0, conv1]), v_rows


def kernel(x_prompt, x_sample, c_prompt, c_sample, cache_k, cache_v, state_conv, ada_w, ada_b, norm_mix, norm_ffn, gmlp_w_in, gmlp_norm_v, gmlp_w_s, gmlp_b_s, gmlp_w_out, kv_ada_w, kv_ada_b, kv_norm, w_kv, sb_w_q, sb_w_o, ffn_w_up, ffn_conv_w, ffn_conv_b, ffn_w_down, final_norm):
    depth, d = norm_mix.shape
    assert depth == 2 and gmlp_w_in.shape[0] == 1 and sb_w_q.shape[0] == 1
    bp, tp, _ = x_prompt.shape
    bs, ts, _ = x_sample.shape
    assert bp + bs <= ADA_ROWS

    c_rows = jnp.zeros((ADA_ROWS, d), F32).at[:bp].set(c_prompt).at[bp:bp + bs].set(c_sample)
    ada = _ada_call(c_rows, ada_w, ada_b)
    kv_ada = _ada_call(c_rows, kv_ada_w[None], kv_ada_b[None])
    ada = ada.reshape(depth, ADA_ROWS, N_ADA, d)
    kv_rows = jnp.broadcast_to(kv_ada.reshape(1, ADA_ROWS, 2, d), (depth, ADA_ROWS, 2, d))
    mods = jnp.concatenate([ada, kv_rows], axis=2)

    row = lambda a: a.reshape(1, -1)
    wts = {
        "norm_mix0": row(norm_mix[0]), "norm_mix1": row(norm_mix[1]),
        "norm_ffn0": row(norm_ffn[0]), "norm_ffn1": row(norm_ffn[1]),
        "gmlp_w_in": gmlp_w_in[0].astype(BF16), "gmlp_norm_v": row(gmlp_norm_v[0]),
        "gmlp_w_s": gmlp_w_s[0], "gmlp_b_s": gmlp_b_s[0], "gmlp_w_out": gmlp_w_out[0].astype(BF16),
        "kv_norm": row(kv_norm), "w_kv": w_kv.astype(BF16),
        "sb_w_q": sb_w_q[0].astype(BF16), "sb_w_o": sb_w_o[0].astype(BF16),
        "ffn_w_up0": ffn_w_up[0].astype(BF16), "ffn_w_up1": ffn_w_up[1].astype(BF16),
        "ffn_conv_w": ffn_conv_w, "ffn_conv_b0": row(ffn_conv_b[0]), "ffn_conv_b1": row(ffn_conv_b[1]),
        "ffn_w_down0": ffn_w_down[0].astype(BF16), "ffn_w_down1": ffn_w_down[1].astype(BF16),
        "final_norm": row(final_norm),
    }

    conv_zero = jnp.zeros((depth, bp, CONV_W - 1, state_conv.shape[-1]), F32)
    y_p, k_p, v_p, conv_p, _ = _forward(
        x_prompt, mods[:, :bp], conv_zero, None, None, wts,
        tm=min(tp, FFN_ROWS), tm_ffn1=min(tp, FFN1_ROWS), tm_gmlp=min(tp, GMLP_ROWS),
        chunk=min(tp, GMLP_CHUNK), emit_v=False)
    y_s, k_s, v_s, conv_s, gv_s = _forward(
        x_sample, mods[:, bp:bp + bs], state_conv, cache_k, cache_v, wts,
        tm=ts, tm_ffn1=ts, tm_gmlp=ts, chunk=min(ts, GMLP_CHUNK), emit_v=True)
    return (y_p, y_s, k_p, v_p, k_s, v_s, conv_p, conv_s, gv_s[None])
```
